```python
import math
import jax, jax.numpy as jnp
from jax import lax
import numpy as np

D_MODEL = 1024
BATCH = 2
SEQ = 16384
DEPTH = 4

D_MIX = D_MODEL
D_POOL = D_MIX // 2
D_SGU = D_MIX // 2
POOL_WINDOWS = (2, 4, 8, 16)
N_POOL_GROUPS = len(POOL_WINDOWS)
POOL_GROUP_DIM = D_POOL // N_POOL_GROUPS
CHUNK = 128
SGU_HEADS = 4
SGU_HEAD_DIM = D_SGU // SGU_HEADS
D_IN = D_POOL + 2 * D_SGU
D_FF = 2816
CONV_WIDTH = 3
N_MOD = 6
DEEPNORM_ALPHA = (2.0 * DEPTH) ** 0.25
DEEPNORM_BETA = (8.0 * DEPTH) ** -0.25
LN_EPS = 1e-5

kernel_name = "hybrid_pool_sgu_convffn_deepnorm_adaln"


def _layernorm(x, g, b):
    xf = x.astype(jnp.float32)
    mu = jnp.mean(xf, axis=-1, keepdims=True)
    var = jnp.mean(jnp.square(xf - mu), axis=-1, keepdims=True)
    y = (xf - mu) * lax.rsqrt(var + LN_EPS)
    return (y * g.astype(jnp.float32) + b.astype(jnp.float32)).astype(x.dtype)


def _modulate(x, shift, scale):
    return x * (1.0 + scale[:, None, :]) + shift[:, None, :]


def _pool_mixer(a, pool_w, pool_scale):
    B, S, _ = a.shape
    ag = a.reshape(B, S, N_POOL_GROUPS, POOL_GROUP_DIM).astype(jnp.float32)
    cs = jnp.cumsum(ag, axis=1)
    t = jnp.arange(S)
    pooled = []
    for g, w in enumerate(POOL_WINDOWS):
        csg = cs[:, :, g]
        prev = jnp.pad(csg, ((0, 0), (w, 0), (0, 0)))[:, :S]
        cnt = jnp.minimum(t + 1, w).astype(jnp.float32)[None, :, None]
        pooled.append((csg - prev) / cnt)
    pooled = (jnp.stack(pooled, axis=2) - ag).astype(a.dtype)
    mixed = jnp.einsum('bsgc,gcd->bsgd', pooled, pool_w)
    return mixed.reshape(B, S, D_POOL) * pool_scale


def _sgu_mixer(u, v, ln_g, ln_b, sgu_w, sgu_b):
    B, S, _ = v.shape
    u = jax.nn.gelu(u)
    v = _layernorm(jax.nn.gelu(v), ln_g, ln_b)
    vc = v.reshape(B, S // CHUNK, CHUNK, SGU_HEADS, SGU_HEAD_DIM)
    mask = jnp.tril(jnp.ones((CHUNK, CHUNK), dtype=bool))
    w = jnp.where(mask[None], sgu_w, jnp.zeros((), sgu_w.dtype))
    z = jnp.einsum('hts,bcshd->bcthd', w, vc) + jnp.transpose(sgu_b)[None, None, :, :, None]
    return u * z.reshape(B, S, D_SGU)


def _causal_dwconv(h, w, b):
    S = h.shape[1]
    hp = jnp.pad(h, ((0, 0), (CONV_WIDTH - 1, 0), (0, 0)))
    y = b
    for k in range(CONV_WIDTH):
        y = y + hp[:, k:k + S] * w[k]
    return y


def setup_inputs(seed: int = 0) -> dict:
    key = jax.random.key(seed)
    ks = jax.random.split(key, 24)
    nrm = lambda k, shp: jax.random.normal(k, shp, dtype=jnp.float32)
    L, D = DEPTH, D_MODEL
    ada_b = jnp.concatenate([
        0.02 * nrm(ks[3], (L, 2 * D)),
        1.0 + 0.02 * nrm(ks[4], (L, D)),
        0.02 * nrm(ks[5], (L, 2 * D)),
        1.0 + 0.02 * nrm(ks[6], (L, D)),
    ], axis=-1)
    return {
        "x": nrm(ks[0], (BATCH, SEQ, D)),
        "c": nrm(ks[1], (BATCH, D)),
        "ada_w": 0.1 * D ** -0.5 * nrm(ks[2], (L, D, N_MOD * D)),
        "ada_b": ada_b,
        "w_in": D ** -0.5 * nrm(ks[7], (L, D, D_IN)),
        "pool_w": POOL_GROUP_DIM ** -0.5 * nrm(ks[8], (L, N_POOL_GROUPS, POOL_GROUP_DIM, POOL_GROUP_DIM)),
        "pool_scale": 1.0 + 0.02 * nrm(ks[9], (L, D_POOL)),
        "sgu_ln_g": 1.0 + 0.02 * nrm(ks[10], (L, D_SGU)),
        "sgu_ln_b": 0.02 * nrm(ks[11], (L, D_SGU)),
        "sgu_w": CHUNK ** -0.5 * nrm(ks[12], (L, SGU_HEADS, CHUNK, CHUNK)),
        "sgu_b": 1.0 + 0.02 * nrm(ks[13], (L, SGU_HEADS, CHUNK)),
        "w_out": DEEPNORM_BETA * D_MIX ** -0.5 * nrm(ks[14], (L, D_MIX, D)),
        "ln1_g": 1.0 + 0.02 * nrm(ks[15], (L, D)),
        "ln1_b": 0.02 * nrm(ks[16], (L, D)),
        "w_up": D ** -0.5 * nrm(ks[17], (L, D, 2 * D_FF)),
        "conv_w": 0.5 * nrm(ks[18], (L, CONV_WIDTH, D_FF)),
        "conv_b": 0.02 * nrm(ks[19], (L, D_FF)),
        "w_down": DEEPNORM_BETA * D_FF ** -0.5 * nrm(ks[20], (L, D_FF, D)),
        "ln2_g": 1.0 + 0.02 * nrm(ks[21], (L, D)),
        "ln2_b": 0.02 * nrm(ks[22], (L, D)),
    }


def reference(x, c, ada_w, ada_b, w_in, pool_w, pool_scale, sgu_ln_g, sgu_ln_b, sgu_w, sgu_b,
              w_out, ln1_g, ln1_b, w_up, conv_w, conv_b, w_down, ln2_g, ln2_b):
    c_act = jax.nn.silu(c)
    for l in range(DEPTH):
        mod = c_act @ ada_w[l] + ada_b[l]
        shift1, scale1, gate1, shift2, scale2, gate2 = jnp.split(mod, N_MOD, axis=-1)

        h = _modulate(x, shift1, scale1)
        proj = jnp.einsum('bsd,de->bse', h, w_in[l])
        a = proj[..., :D_POOL]
        u = proj[..., D_POOL:D_POOL + D_SGU]
        v = proj[..., D_POOL + D_SGU:]
        y_a = _pool_mixer(a, pool_w[l], pool_scale[l])
        y_b = _sgu_mixer(u, v, sgu_ln_g[l], sgu_ln_b[l], sgu_w[l], sgu_b[l])
        mix = jnp.concatenate([y_a, y_b], axis=-1)
        f = jnp.einsum('bse,ed->bsd', mix, w_out[l])
        x = _layernorm(DEEPNORM_ALPHA * x + gate1[:, None, :] * f, ln1_g[l], ln1_b[l])

        h = _modulate(x, shift2, scale2)
        up = jnp.einsum('bsd,df->bsf', h, w_up[l])
        g, val = up[..., :D_FF], up[..., D_FF:]
        g = _causal_dwconv(g, conv_w[l], conv_b[l])
        f = jnp.einsum('bsf,fd->bsd', jax.nn.gelu(g) * val, w_down[l])
        x = _layernorm(DEEPNORM_ALPHA * x + gate2[:, None, :] * f, ln2_g[l], ln2_b[l])
    return x
```

```python
import functools
import math

import jax
import jax.numpy as jnp
from jax import lax
from jax.experimental import pallas as pl
from jax.experimental.pallas import tpu as pltpu

LANES = 128
SUBLANES = 8

POOL_WINDOWS = (2, 4, 8, 16)
POOL_HALO = 16
CHUNK = 128
CONV_WIDTH = 3
N_MOD = 6
LN_EPS = 1e-5
FF_CHUNK = 256

TM_MIX = 512
TM_FFN = 512
TN_MOD = 1536
VMEM_LIMIT_BYTES = 56 * 1024 * 1024

_GELU_C0 = math.sqrt(2.0 / math.pi)
_GELU_C1 = 0.044715 * _GELU_C0


def _gelu(x):
    inner = x * (_GELU_C0 + _GELU_C1 * (x * x))
    hx = 0.5 * x
    return hx + hx * jnp.tanh(inner)


def _layernorm(y, g, b):
    mu = jnp.mean(y, axis=-1, keepdims=True)
    yc = y - mu
    var = jnp.mean(yc * yc, axis=-1, keepdims=True)
    return yc * lax.rsqrt(var + LN_EPS) * g + b


def _mod_kernel(cb_ref, w_ref, b_ref, o_ref):
    n_b = cb_ref.shape[0]
    tn = w_ref.shape[2]
    for b in range(n_b):
        cb = cb_ref[b]
        ca = cb / (1.0 + jnp.exp(-cb))
        for j in range(tn // LANES):
            cols = slice(j * LANES, (j + 1) * LANES)
            r = jnp.sum(w_ref[0, :, cols] * ca, axis=0, keepdims=True)
            o_ref[0, b:b + 1, cols] = r + b_ref[0, :, cols]


def _modulation(c, ada_w, ada_b):
    n_layers, d, n_out = ada_w.shape
    n_b = c.shape[0]
    cb = jnp.broadcast_to(c[:, :, None], (n_b, d, LANES))
    return pl.pallas_call(
        _mod_kernel,
        grid=(n_layers, n_out // TN_MOD),
        in_specs=[
            pl.BlockSpec((n_b, d, LANES), lambda l, j: (0, 0, 0)),
            pl.BlockSpec((1, d, TN_MOD), lambda l, j: (l, 0, j)),
            pl.BlockSpec((1, 1, TN_MOD), lambda l, j: (l, 0, j)),
        ],
        out_specs=pl.BlockSpec((1, n_b, TN_MOD), lambda l, j: (l, 0, j)),
        out_shape=jax.ShapeDtypeStruct((n_layers, n_b, n_out), jnp.float32),
        compiler_params=pltpu.CompilerParams(
            dimension_semantics=("arbitrary", "arbitrary"),
            vmem_limit_bytes=VMEM_LIMIT_BYTES),
        name="adaln_modulation",
    )(cb, ada_w, ada_b.reshape(n_layers, 1, n_out))


def _mix_kernel(alpha, x_ref, mod_ref, w_in_ref, pool_w_ref, pool_scale_ref, sgu_g_ref, sgu_b_ref,
                sgu_w_ref, sgu_bias_ref, w_out_ref, ln_g_ref, ln_b_ref, o_ref, ext_ref, mix_ref):
    i = pl.program_id(1)
    tm = x_ref.shape[1]
    d_pool = ext_ref.shape[1]
    d_sgu = sgu_g_ref.shape[1]
    gdim = d_pool // len(POOL_WINDOWS)
    n_heads = sgu_w_ref.shape[0]
    hdim = d_sgu // n_heads

    @pl.when(i == 0)
    def _():
        ext_ref[0:POOL_HALO, :] = jnp.zeros((POOL_HALO, d_pool), jnp.float32)

    x = x_ref[0]
    shift = mod_ref[0, 0:1, :]
    scale = mod_ref[0, 1:2, :]
    gate = mod_ref[0, 2:3, :]
    h = (x * (1.0 + scale) + shift).astype(jnp.bfloat16)
    proj = jnp.dot(h, w_in_ref[...], preferred_element_type=jnp.float32)

    ext_ref[POOL_HALO:, :] = proj[:, :d_pool]
    t = i * tm + lax.broadcasted_iota(jnp.int32, (tm, gdim), 0)
    for g, w in enumerate(POOL_WINDOWS):
        cols = slice(g * gdim, (g + 1) * gdim)
        s = ext_ref[:, cols]
        a_g = s[POOL_HALO:]
        sh = 1
        while sh < w:
            s = s + pltpu.roll(s, sh, 0)
            sh *= 2
        cnt = jnp.minimum(t + 1, w).astype(jnp.float32)
        pooled = s[POOL_HALO:] / cnt - a_g
        mixed = jnp.dot(pooled.astype(jnp.bfloat16), pool_w_ref[g], preferred_element_type=jnp.float32)
        mix_ref[:, cols] = (mixed * pool_scale_ref[:, cols]).astype(jnp.bfloat16)
    ext_ref[0:POOL_HALO, :] = ext_ref[tm:tm + POOL_HALO, :]

    u = _gelu(proj[:, d_pool:d_pool + d_sgu])
    v = _gelu(proj[:, d_pool + d_sgu:])
    vn = _layernorm(v, sgu_g_ref[...], sgu_b_ref[...]).astype(jnp.bfloat16)
    row = lax.broadcasted_iota(jnp.int32, (CHUNK, CHUNK), 0)
    col = lax.broadcasted_iota(jnp.int32, (CHUNK, CHUNK), 1)
    tril = row >= col
    for hd in range(n_heads):
        hcols = slice(hd * hdim, (hd + 1) * hdim)
        w_tril = jnp.where(tril, sgu_w_ref[hd], 0.0).astype(jnp.bfloat16)
        bias = sgu_bias_ref[hd]
        for c in range(tm // CHUNK):
            rows = slice(c * CHUNK, (c + 1) * CHUNK)
            z = jnp.dot(w_tril, vn[rows, hcols], preferred_element_type=jnp.float32) + bias
            mix_ref[rows, d_pool + hd * hdim:d_pool + (hd + 1) * hdim] = (u[rows, hcols] * z).astype(jnp.bfloat16)

    f = jnp.dot(mix_ref[...], w_out_ref[...], preferred_element_type=jnp.float32)
    o_ref[0] = _layernorm(alpha * x + gate * f, ln_g_ref[...], ln_b_ref[...])


def _const_spec(shape):
    nd = len(shape)
    return pl.BlockSpec(shape, lambda b, i: (0,) * nd, pipeline_mode=pl.Buffered(1))


def _token_mixing(x, mod, w_in, pool_w, pool_scale, sgu_g, sgu_b, sgu_w, sgu_bias, w_out, ln_g, ln_b, alpha):
    n_b, seq, d = x.shape
    d_pool = pool_scale.shape[1]
    tm = TM_MIX
    return pl.pallas_call(
        functools.partial(_mix_kernel, alpha),
        grid=(n_b, seq // tm),
        in_specs=[
            pl.BlockSpec((1, tm, d), lambda b, i: (b, i, 0)),
            pl.BlockSpec((1, N_MOD, d), lambda b, i: (b, 0, 0)),
            _const_spec(w_in.shape), _const_spec(pool_w.shape), _const_spec(pool_scale.shape),
            _const_spec(sgu_g.shape), _const_spec(sgu_b.shape), _const_spec(sgu_w.shape),
            _const_spec(sgu_bias.shape), _const_spec(w_out.shape), _const_spec(ln_g.shape),
            _const_spec(ln_b.shape),
        ],
        out_specs=pl.BlockSpec((1, tm, d), lambda b, i: (b, i, 0)),
        out_shape=jax.ShapeDtypeStruct(x.shape, x.dtype),
        scratch_shapes=[
            pltpu.VMEM((POOL_HALO + tm, d_pool), jnp.float32),
            pltpu.VMEM((tm, w_out.shape[0]), jnp.bfloat16),
        ],
        compiler_params=pltpu.CompilerParams(
            dimension_semantics=("arbitrary", "arbitrary"),
            vmem_limit_bytes=VMEM_LIMIT_BYTES),
        name="token_mixing",
    )(x, mod, w_in, pool_w, pool_scale, sgu_g, sgu_b, sgu_w, sgu_bias, w_out, ln_g, ln_b)


def _ffn_kernel(alpha, x_ref, mod_ref, w_up_ref, conv_w_ref, conv_b_ref, w_down_ref, ln_g_ref, ln_b_ref,
                o_ref, h_ref, gext_ref, carry_ref, acc_ref):
    i = pl.program_id(1)
    tm = x_ref.shape[1]
    n_chunks = w_up_ref.shape[0]
    fc = w_down_ref.shape[1]

    @pl.when(i == 0)
    def _():
        carry_ref[...] = jnp.zeros(carry_ref.shape, jnp.float32)

    x = x_ref[0]
    shift = mod_ref[0, 3:4, :]
    scale = mod_ref[0, 4:5, :]
    gate = mod_ref[0, 5:6, :]
    h_ref[...] = (x * (1.0 + scale) + shift).astype(jnp.bfloat16)
    acc_ref[...] = jnp.zeros(acc_ref.shape, jnp.float32)

    def chunk(c, carry):
        up = jnp.dot(h_ref[...], w_up_ref[c], preferred_element_type=jnp.float32)
        g = up[:, :fc]
        val = up[:, fc:]
        gext_ref[0:SUBLANES, :] = carry_ref[c]
        gext_ref[SUBLANES:, :] = g
        carry_ref[c] = g[tm - SUBLANES:, :]
        g1 = gext_ref[SUBLANES - 1:SUBLANES - 1 + tm, :]
        g2 = gext_ref[SUBLANES - 2:SUBLANES - 2 + tm, :]
        cw = conv_w_ref[c]
        y = conv_b_ref[c] + g2 * cw[0:1, :] + g1 * cw[1:2, :] + g * cw[2:3, :]
        act = (_gelu(y) * val).astype(jnp.bfloat16)
        acc_ref[...] += jnp.dot(act, w_down_ref[c], preferred_element_type=jnp.float32)
        return carry

    lax.fori_loop(0, n_chunks, chunk, 0)
    o_ref[0] = _layernorm(alpha * x + gate * acc_ref[...], ln_g_ref[...], ln_b_ref[...])


def _channel_mixing(x, mod, w_up, conv_w, conv_b, w_down, ln_g, ln_b, alpha):
    n_b, seq, d = x.shape
    n_chunks, _, fc2 = w_up.shape
    fc = fc2 // 2
    tm = TM_FFN
    return pl.pallas_call(
        functools.partial(_ffn_kernel, alpha),
        grid=(n_b, seq // tm),
        in_specs=[
            pl.BlockSpec((1, tm, d), lambda b, i: (b, i, 0)),
            pl.BlockSpec((1, N_MOD, d), lambda b, i: (b, 0, 0)),
            _const_spec(w_up.shape), _const_spec(conv_w.shape), _const_spec(conv_b.shape),
            _const_spec(w_down.shape), _const_spec(ln_g.shape), _const_spec(ln_b.shape),
        ],
        out_specs=pl.BlockSpec((1, tm, d), lambda b, i: (b, i, 0)),
        out_shape=jax.ShapeDtypeStruct(x.shape, x.dtype),
        scratch_shapes=[
            pltpu.VMEM((tm, d), jnp.bfloat16),
            pltpu.VMEM((SUBLANES + tm, fc), jnp.float32),
            pltpu.VMEM((n_chunks, SUBLANES, fc), jnp.float32),
            pltpu.VMEM((tm, d), jnp.float32),
        ],
        compiler_params=pltpu.CompilerParams(
            dimension_semantics=("arbitrary", "arbitrary"),
            vmem_limit_bytes=VMEM_LIMIT_BYTES),
        name="channel_mixing",
    )(x, mod, w_up, conv_w, conv_b, w_down, ln_g, ln_b)


def kernel(x, c, ada_w, ada_b, w_in, pool_w, pool_scale, sgu_ln_g, sgu_ln_b, sgu_w, sgu_b, w_out, ln1_g, ln1_b,
           w_up, conv_w, conv_b, w_down, ln2_g, ln2_b):
    n_layers, d, _ = ada_w.shape
    n_b = x.shape[0]
    d_ff = w_down.shape[1]
    n_heads = sgu_w.shape[1]
    n_chunks = d_ff // FF_CHUNK
    alpha = (2.0 * n_layers) ** 0.25
    bf16 = jnp.bfloat16

    mod = _modulation(c, ada_w, ada_b).reshape(n_layers, n_b, N_MOD, d)

    w_in_b = w_in.astype(bf16)
    pool_w_b = pool_w.astype(bf16)
    w_out_b = w_out.astype(bf16)
    w_up_g = w_up[:, :, :d_ff].reshape(n_layers, d, n_chunks, FF_CHUNK)
    w_up_v = w_up[:, :, d_ff:].reshape(n_layers, d, n_chunks, FF_CHUNK)
    w_up_b = jnp.transpose(jnp.concatenate([w_up_g, w_up_v], axis=-1), (0, 2, 1, 3)).astype(bf16)
    w_down_b = w_down.reshape(n_layers, n_chunks, FF_CHUNK, d).astype(bf16)
    conv_w_c = jnp.transpose(conv_w.reshape(n_layers, CONV_WIDTH, n_chunks, FF_CHUNK), (0, 2, 1, 3))
    conv_b_c = conv_b.reshape(n_layers, n_chunks, 1, FF_CHUNK)
    sgu_bias = jnp.broadcast_to(sgu_b[:, :, :, None], (n_layers, n_heads, CHUNK, d // (2 * n_heads)))

    row = lambda p, l: p[l].reshape(1, -1)
    for l in range(n_layers):
        x = _token_mixing(x, mod[l], w_in_b[l], pool_w_b[l], row(pool_scale, l), row(sgu_ln_g, l),
                          row(sgu_ln_b, l), sgu_w[l], sgu_bias[l], w_out_b[l], row(ln1_g, l), row(ln1_b, l), alpha)
        x = _channel_mixing(x, mod[l], w_up_b[l], conv_w_c[l], conv_b_c[l], w_down_b[l],
                            row(ln2_g, l), row(ln2_b, l), alpha)
    return x
```

```python
import functools
import math

import jax
import jax.numpy as jnp
from jax import lax
from jax.experimental import pallas as pl
from jax.experimental.pallas import tpu as pltpu

LANES = 128
SUBLANES = 8

POOL_WINDOWS = (2, 4, 8, 16)
POOL_HALO = 16
CHUNK = 128
CONV_WIDTH = 3
N_MOD = 6
LN_EPS = 1e-5
FF_CHUNK = 256

TM_MIX = 512
TM_FFN = 512
FFN_ROW_BLOCK = 128
TN_MOD = 1536
VMEM_LIMIT_BYTES = 56 * 1024 * 1024

_GELU_C0 = math.sqrt(2.0 / math.pi)
_GELU_C1 = 0.044715 * _GELU_C0


def _gelu(x):
    inner = x * (_GELU_C0 + _GELU_C1 * (x * x))
    hx = 0.5 * x
    return hx + hx * jnp.tanh(inner)


def _layernorm(y, g, b):
    mu = jnp.mean(y, axis=-1, keepdims=True)
    yc = y - mu
    var = jnp.mean(yc * yc, axis=-1, keepdims=True)
    return yc * lax.rsqrt(var + LN_EPS) * g + b


def _mod_kernel(cb_ref, w_ref, b_ref, o_ref):
    n_b = cb_ref.shape[0]
    tn = w_ref.shape[2]
    for b in range(n_b):
        cb = cb_ref[b]
        ca = cb / (1.0 + jnp.exp(-cb))
        for j in range(tn // LANES):
            cols = slice(j * LANES, (j + 1) * LANES)
            r = jnp.sum(w_ref[0, :, cols] * ca, axis=0, keepdims=True)
            o_ref[0, b:b + 1, cols] = r + b_ref[0, :, cols]


def _modulation(c, ada_w, ada_b):
    n_layers, d, n_out = ada_w.shape
    n_b = c.shape[0]
    cb = jnp.broadcast_to(c[:, :, None], (n_b, d, LANES))
    return pl.pallas_call(
        _mod_kernel,
        grid=(n_layers, n_out // TN_MOD),
        in_specs=[
            pl.BlockSpec((n_b, d, LANES), lambda l, j: (0, 0, 0)),
            pl.BlockSpec((1, d, TN_MOD), lambda l, j: (l, 0, j)),
            pl.BlockSpec((1, 1, TN_MOD), lambda l, j: (l, 0, j)),
        ],
        out_specs=pl.BlockSpec((1, n_b, TN_MOD), lambda l, j: (l, 0, j)),
        out_shape=jax.ShapeDtypeStruct((n_layers, n_b, n_out), jnp.float32),
        compiler_params=pltpu.CompilerParams(
            dimension_semantics=("arbitrary", "arbitrary"),
            vmem_limit_bytes=VMEM_LIMIT_BYTES),
        name="adaln_modulation",
    )(cb, ada_w, ada_b.reshape(n_layers, 1, n_out))


def _mix_kernel(alpha, x_ref, mod_ref, w_in_ref, pool_w_ref, pool_scale_ref, sgu_g_ref, sgu_b_ref,
                sgu_w_ref, sgu_bias_ref, w_out_ref, ln_g_ref, ln_b_ref, o_ref, ext_ref, mix_ref):
    i = pl.program_id(1)
    tm = x_ref.shape[1]
    d_pool = ext_ref.shape[1]
    d_sgu = sgu_g_ref.shape[1]
    gdim = d_pool // len(POOL_WINDOWS)
    n_heads = sgu_w_ref.shape[0]
    hdim = d_sgu // n_heads

    @pl.when(i == 0)
    def _():
        ext_ref[0:POOL_HALO, :] = jnp.zeros((POOL_HALO, d_pool), jnp.float32)

    x = x_ref[0]
    shift = mod_ref[0, 0:1, :]
    scale = mod_ref[0, 1:2, :]
    gate = mod_ref[0, 2:3, :]
    h = (x * (1.0 + scale) + shift).astype(jnp.bfloat16)
    proj = jnp.dot(h, w_in_ref[...], preferred_element_type=jnp.float32)

    ext_ref[POOL_HALO:, :] = proj[:, :d_pool]
    t = i * tm + lax.broadcasted_iota(jnp.int32, (tm, gdim), 0)
    for g, w in enumerate(POOL_WINDOWS):
        cols = slice(g * gdim, (g + 1) * gdim)
        s = ext_ref[:, cols]
        a_g = s[POOL_HALO:]
        sh = 1
        while sh < w:
            s = s + pltpu.roll(s, sh, 0)
            sh *= 2
        cnt = jnp.minimum(t + 1, w).astype(jnp.float32)
        pooled = s[POOL_HALO:] / cnt - a_g
        mixed = jnp.dot(pooled.astype(jnp.bfloat16), pool_w_ref[g], preferred_element_type=jnp.float32)
        mix_ref[:, cols] = (mixed * pool_scale_ref[:, cols]).astype(jnp.bfloat16)
    ext_ref[0:POOL_HALO, :] = ext_ref[tm:tm + POOL_HALO, :]

    u = _gelu(proj[:, d_pool:d_pool + d_sgu])
    v = _gelu(proj[:, d_pool + d_sgu:])
    vn = _layernorm(v, sgu_g_ref[...], sgu_b_ref[...]).astype(jnp.bfloat16)
    row = lax.broadcasted_iota(jnp.int32, (CHUNK, CHUNK), 0)
    col = lax.broadcasted_iota(jnp.int32, (CHUNK, CHUNK), 1)
    tril = row >= col
    for hd in range(n_heads):
        hcols = slice(hd * hdim, (hd + 1) * hdim)
        w_tril = jnp.where(tril, sgu_w_ref[hd], 0.0).astype(jnp.bfloat16)
        bias = sgu_bias_ref[hd]
        for c in range(tm // CHUNK):
            rows = slice(c * CHUNK, (c + 1) * CHUNK)
            z = jnp.dot(w_tril, vn[rows, hcols], preferred_element_type=jnp.float32) + bias
            mix_ref[rows, d_pool + hd * hdim:d_pool + (hd + 1) * hdim] = (u[rows, hcols] * z).astype(jnp.bfloat16)

    f = jnp.dot(mix_ref[...], w_out_ref[...], preferred_element_type=jnp.float32)
    o_ref[0] = _layernorm(alpha * x + gate * f, ln_g_ref[...], ln_b_ref[...])


def _const_spec(shape):
    nd = len(shape)
    return pl.BlockSpec(shape, lambda b, i: (0,) * nd, pipeline_mode=pl.Buffered(1))


def _token_mixing(x, mod, w_in, pool_w, pool_scale, sgu_g, sgu_b, sgu_w, sgu_bias, w_out, ln_g, ln_b, alpha):
    n_b, seq, d = x.shape
    d_pool = pool_scale.shape[1]
    tm = TM_MIX
    return pl.pallas_call(
        functools.partial(_mix_kernel, alpha),
        grid=(n_b, seq // tm),
        in_specs=[
            pl.BlockSpec((1, tm, d), lambda b, i: (b, i, 0)),
            pl.BlockSpec((1, N_MOD, d), lambda b, i: (b, 0, 0)),
            _const_spec(w_in.shape), _const_spec(pool_w.shape), _const_spec(pool_scale.shape),
            _const_spec(sgu_g.shape), _const_spec(sgu_b.shape), _const_spec(sgu_w.shape),
            _const_spec(sgu_bias.shape), _const_spec(w_out.shape), _const_spec(ln_g.shape),
            _const_spec(ln_b.shape),
        ],
        out_specs=pl.BlockSpec((1, tm, d), lambda b, i: (b, i, 0)),
        out_shape=jax.ShapeDtypeStruct(x.shape, x.dtype),
        scratch_shapes=[
            pltpu.VMEM((POOL_HALO + tm, d_pool), jnp.float32),
            pltpu.VMEM((tm, w_out.shape[0]), jnp.bfloat16),
        ],
        compiler_params=pltpu.CompilerParams(
            dimension_semantics=("arbitrary", "arbitrary"),
            vmem_limit_bytes=VMEM_LIMIT_BYTES),
        name="token_mixing",
    )(x, mod, w_in, pool_w, pool_scale, sgu_g, sgu_b, sgu_w, sgu_bias, w_out, ln_g, ln_b)


def _shift_rows(g, prev, k):
    rolled = pltpu.roll(g, k, 0)
    head = lax.broadcasted_iota(jnp.int32, (SUBLANES, g.shape[1]), 0) < k
    top = jnp.where(head, pltpu.roll(prev, k, 0), rolled[:SUBLANES])
    return jnp.concatenate([top, rolled[SUBLANES:]], axis=0)


def _ffn_kernel(alpha, x_ref, mod_ref, w_up_ref, conv_w_ref, conv_b_ref, w_down_ref, ln_g_ref, ln_b_ref,
                o_ref, h_ref, carry_ref, acc_ref):
    i = pl.program_id(1)
    tm = x_ref.shape[1]
    n_chunks = w_up_ref.shape[0]
    fc = w_down_ref.shape[1]
    rb = FFN_ROW_BLOCK
    n_rb = tm // rb

    @pl.when(i == 0)
    def _():
        carry_ref[...] = jnp.zeros(carry_ref.shape, jnp.float32)

    shift = mod_ref[0, 3:4, :]
    scale = mod_ref[0, 4:5, :]
    gate = mod_ref[0, 5:6, :]

    def rows(r):
        return slice(r * rb, (r + 1) * rb)

    def modulate(r):
        h_ref[rows(r), :] = (x_ref[0, rows(r), :] * (1.0 + scale) + shift).astype(jnp.bfloat16)

    def up_proj(r, c):
        return jnp.dot(h_ref[rows(r), :], w_up_ref[c], preferred_element_type=jnp.float32)

    def activate(c, up):
        g = up[:, :fc]
        val = up[:, fc:]
        prev = carry_ref[c]
        carry_ref[c] = g[rb - SUBLANES:, :]
        cw = conv_w_ref[c]
        y = (conv_b_ref[c] + _shift_rows(g, prev, 2) * cw[0:1, :] + _shift_rows(g, prev, 1) * cw[1:2, :]
             + g * cw[2:3, :])
        return (_gelu(y) * val).astype(jnp.bfloat16)

    def down_proj(r, c, act):
        d = jnp.dot(act, w_down_ref[c], preferred_element_type=jnp.float32)
        if c == 0:
            acc_ref[rows(r), :] = d
        else:
            acc_ref[rows(r), :] += d

    def finish(r):
        y = alpha * x_ref[0, rows(r), :] + gate * acc_ref[rows(r), :]
        o_ref[0, rows(r), :] = _layernorm(y, ln_g_ref[...], ln_b_ref[...])

    steps = [(r, c) for r in range(n_rb) for c in range(n_chunks)]
    modulate(0)
    pending = up_proj(*steps[0])
    for k, (r, c) in enumerate(steps):
        nxt = None
        if k + 1 < len(steps):
            r1, c1 = steps[k + 1]
            if c1 == 0:
                modulate(r1)
            nxt = up_proj(r1, c1)
        act = activate(c, pending)
        down_proj(r, c, act)
        if c == n_chunks - 1:
            finish(r)
        pending = nxt


def _channel_mixing(x, mod, w_up, conv_w, conv_b, w_down, ln_g, ln_b, alpha):
    n_b, seq, d = x.shape
    n_chunks, _, fc2 = w_up.shape
    fc = fc2 // 2
    tm = TM_FFN
    return pl.pallas_call(
        functools.partial(_ffn_kernel, alpha),
        grid=(n_b, seq // tm),
        in_specs=[
            pl.BlockSpec((1, tm, d), lambda b, i: (b, i, 0)),
            pl.BlockSpec((1, N_MOD, d), lambda b, i: (b, 0, 0)),
            _const_spec(w_up.shape), _const_spec(conv_w.shape), _const_spec(conv_b.shape),
            _const_spec(w_down.shape), _const_spec(ln_g.shape), _const_spec(ln_b.shape),
        ],
        out_specs=pl.BlockSpec((1, tm, d), lambda b, i: (b, i, 0)),
        out_shape=jax.ShapeDtypeStruct(x.shape, x.dtype),
        scratch_shapes=[
            pltpu.VMEM((tm, d), jnp.bfloat16),
            pltpu.VMEM((n_chunks, SUBLANES, fc), jnp.float32),
            pltpu.VMEM((tm, d), jnp.float32),
        ],
        compiler_params=pltpu.CompilerParams(
            dimension_semantics=("arbitrary", "arbitrary"),
            vmem_limit_bytes=VMEM_LIMIT_BYTES),
        name="channel_mixing",
    )(x, mod, w_up, conv_w, conv_b, w_down, ln_g, ln_b)


def kernel(x, c, ada_w, ada_b, w_in, pool_w, pool_scale, sgu_ln_g, sgu_ln_b, sgu_w, sgu_b, w_out, ln1_g, ln1_b,
           w_up, conv_w, conv_b, w_down, ln2_g, ln2_b):
    n_layers, d, _ = ada_w.shape
    n_b = x.shape[0]
    d_ff = w_down.shape[1]
    n_heads = sgu_w.shape[1]
    n_chunks = d_ff // FF_CHUNK
    alpha = (2.0 * n_layers) ** 0.25
    bf16 = jnp.bfloat16

    mod = _modulation(c, ada_w, ada_b).reshape(n_layers, n_b, N_MOD, d)

    w_in_b = w_in.astype(bf16)
    pool_w_b = pool_w.astype(bf16)
    w_out_b = w_out.astype(bf16)
    w_up_g = w_up[:, :, :d_ff].reshape(n_layers, d, n_chunks, FF_CHUNK)
    w_up_v = w_up[:, :, d_ff:].reshape(n_layers, d, n_chunks, FF_CHUNK)
    w_up_b = jnp.transpose(jnp.concatenate([w_up_g, w_up_v], axis=-1), (0, 2, 1, 3)).astype(bf16)
    w_down_b = w_down.reshape(n_layers, n_chunks, FF_CHUNK, d).astype(bf16)
    conv_w_c = jnp.transpose(conv_w.reshape(n_layers, CONV_WIDTH, n_chunks, FF_CHUNK), (0, 2, 1, 3))
    conv_b_c = conv_b.reshape(n_layers, n_chunks, 1, FF_CHUNK)
    sgu_bias = jnp.broadcast_to(sgu_b[:, :, :, None], (n_layers, n_heads, CHUNK, d // (2 * n_heads)))

    row = lambda p, l: p[l].reshape(1, -1)
    for l in range(n_layers):
        x = _token_mixing(x, mod[l], w_in_b[l], pool_w_b[l], row(pool_scale, l), row(sgu_ln_g, l),
                          row(sgu_ln_b, l), sgu_w[l], sgu_bias[l], w_out_b[l], row(ln1_g, l), row(ln1_b, l), alpha)
        x = _channel_mixing(x, mod[l], w_up_b[l], conv_w_c[l], conv_b_c[l], w_down_b[l],
                            row(ln2_g, l), row(ln2_b, l), alpha)
    return x
```

```python
import functools
import math

import jax
import jax.numpy as jnp
from jax import lax
from jax.experimental import pallas as pl
from jax.experimental.pallas import tpu as pltpu

LANES = 128
SUBLANES = 8

POOL_WINDOWS = (2, 4, 8, 16)
POOL_HALO = 16
CHUNK = 128
CONV_WIDTH = 3
N_MOD = 6
LN_EPS = 1e-5
FF_CHUNK = 256
N_HALVES = 2

TM_MIX = 1024
TM_FFN = 512
FFN_ROW_BLOCK = 128
TN_MOD = 1536
VMEM_LIMIT_BYTES = 56 * 1024 * 1024

_GELU_C0 = math.sqrt(2.0 / math.pi)
_GELU_C1 = 0.044715 * _GELU_C0


def _gelu(x):
    inner = x * (_GELU_C0 + _GELU_C1 * (x * x))
    hx = 0.5 * x
    return hx + hx * jnp.tanh(inner)


def _layernorm(y, g, b):
    mu = jnp.mean(y, axis=-1, keepdims=True)
    yc = y - mu
    var = jnp.mean(yc * yc, axis=-1, keepdims=True)
    return yc * lax.rsqrt(var + LN_EPS) * g + b


def _mod_kernel(cb_ref, w_ref, b_ref, o_ref):
    n_b = cb_ref.shape[0]
    tn = w_ref.shape[2]
    for b in range(n_b):
        cb = cb_ref[b]
        ca = cb / (1.0 + jnp.exp(-cb))
        for j in range(tn // LANES):
            cols = slice(j * LANES, (j + 1) * LANES)
            r = jnp.sum(w_ref[0, :, cols] * ca, axis=0, keepdims=True)
            o_ref[0, b:b + 1, cols] = r + b_ref[0, :, cols]


def _modulation(c, ada_w, ada_b):
    n_layers, d, n_out = ada_w.shape
    n_b = c.shape[0]
    cb = jnp.broadcast_to(c[:, :, None], (n_b, d, LANES))
    return pl.pallas_call(
        _mod_kernel,
        grid=(n_layers, n_out // TN_MOD),
        in_specs=[
            pl.BlockSpec((n_b, d, LANES), lambda l, j: (0, 0, 0)),
            pl.BlockSpec((1, d, TN_MOD), lambda l, j: (l, 0, j)),
            pl.BlockSpec((1, 1, TN_MOD), lambda l, j: (l, 0, j)),
        ],
        out_specs=pl.BlockSpec((1, n_b, TN_MOD), lambda l, j: (l, 0, j)),
        out_shape=jax.ShapeDtypeStruct((n_layers, n_b, n_out), jnp.float32),
        compiler_params=pltpu.CompilerParams(
            dimension_semantics=("arbitrary", "arbitrary"),
            vmem_limit_bytes=VMEM_LIMIT_BYTES),
        name="adaln_modulation",
    )(cb, ada_w, ada_b.reshape(n_layers, 1, n_out))


def _mix_kernel(alpha, x_ref, mod_ref, w_in_ref, pool_w_ref, pool_scale_ref, sgu_g_ref, sgu_b_ref,
                sgu_w_ref, sgu_bias_ref, w_out_ref, ln_g_ref, ln_b_ref, o_ref,
                proj_ref, mix_ref, z_ref, wcat_ref, vdiag_ref):
    i = pl.program_id(1)
    tm = x_ref.shape[1]
    d_pool = pool_scale_ref.shape[1]
    d_sgu = sgu_g_ref.shape[1]
    gdim = d_pool // len(POOL_WINDOWS)
    n_heads = sgu_w_ref.shape[0]
    hdim = d_sgu // n_heads
    rb = CHUNK
    n_rb = tm // rb
    n_pairs = n_heads // 2

    @pl.when(i == 0)
    def _():
        proj_ref[0:POOL_HALO, :] = jnp.zeros((POOL_HALO, proj_ref.shape[1]), jnp.float32)
        tril = (lax.broadcasted_iota(jnp.int32, (CHUNK, CHUNK), 0)
                >= lax.broadcasted_iota(jnp.int32, (CHUNK, CHUNK), 1))
        for hd in range(n_heads):
            wcat_ref[hd // 2, :, (hd % 2) * CHUNK:(hd % 2 + 1) * CHUNK] = (
                jnp.where(tril, sgu_w_ref[hd], 0.0).astype(jnp.bfloat16))
        vdiag_ref[...] = jnp.zeros(vdiag_ref.shape, jnp.bfloat16)

    shift = mod_ref[0, 0:1, :]
    scale = mod_ref[0, 1:2, :]
    gate = mod_ref[0, 2:3, :]

    def rows(r):
        return slice(r * rb, (r + 1) * rb)

    def in_proj(r):
        h = (x_ref[0, rows(r), :] * (1.0 + scale) + shift).astype(jnp.bfloat16)
        proj_ref[POOL_HALO + r * rb:POOL_HALO + (r + 1) * rb, :] = jnp.dot(
            h, w_in_ref[...], preferred_element_type=jnp.float32)

    def mixers(r):
        pooled = []
        for g, w in enumerate(POOL_WINDOWS):
            s = proj_ref[r * rb:r * rb + POOL_HALO + rb, g * gdim:(g + 1) * gdim]
            a_g = s[POOL_HALO:]
            sh = 1
            while sh < w:
                s = s + pltpu.roll(s, sh, 0)
                sh *= 2
            s = s[POOL_HALO:]
            if r == 0:
                t = i * tm + lax.broadcasted_iota(jnp.int32, (POOL_HALO, gdim), 0)
                cnt = jnp.minimum(t + 1, w).astype(jnp.float32)
                mean = jnp.concatenate([s[:POOL_HALO] / cnt, s[POOL_HALO:] * (1.0 / w)], axis=0)
            else:
                mean = s * (1.0 / w)
            pooled.append((mean - a_g).astype(jnp.bfloat16))
        for p in range(len(POOL_WINDOWS) // 2):
            cols = slice(2 * p * gdim, 2 * (p + 1) * gdim)
            lhs = jnp.concatenate(pooled[2 * p:2 * p + 2], axis=1)
            mixed = jnp.dot(lhs, pool_w_ref[p], preferred_element_type=jnp.float32)
            mix_ref[rows(r), cols] = (mixed * pool_scale_ref[:, cols]).astype(jnp.bfloat16)
        v = _gelu(proj_ref[POOL_HALO + r * rb:POOL_HALO + (r + 1) * rb, d_pool + d_sgu:])
        vn = _layernorm(v, sgu_g_ref[...], sgu_b_ref[...]).astype(jnp.bfloat16)
        for p in range(n_pairs):
            for q in range(2):
                hd = 2 * p + q
                vdiag_ref[p, q * CHUNK:(q + 1) * CHUNK, q * hdim:(q + 1) * hdim] = vn[:, hd * hdim:(hd + 1) * hdim]
            z_ref[r % 2, :, 2 * p * hdim:2 * (p + 1) * hdim] = jnp.dot(
                wcat_ref[p], vdiag_ref[p], preferred_element_type=jnp.float32)

    def out_proj(r):
        u = _gelu(proj_ref[POOL_HALO + r * rb:POOL_HALO + (r + 1) * rb, d_pool:d_pool + d_sgu])
        mix_ref[rows(r), d_pool:] = (u * (z_ref[r % 2] + sgu_bias_ref[...])).astype(jnp.bfloat16)
        mix = mix_ref[rows(r), :]
        f = jnp.concatenate([jnp.dot(mix, w_out_ref[j], preferred_element_type=jnp.float32)
                             for j in range(w_out_ref.shape[0])], axis=1)
        y = alpha * x_ref[0, rows(r), :] + gate * f
        o_ref[0, rows(r), :] = _layernorm(y, ln_g_ref[...], ln_b_ref[...])

    in_proj(0)
    for k in range(n_rb + 1):
        if k + 1 < n_rb:
            in_proj(k + 1)
        if k < n_rb:
            mixers(k)
        if k >= 1:
            out_proj(k - 1)
    proj_ref[0:POOL_HALO, 0:d_pool] = proj_ref[tm:tm + POOL_HALO, 0:d_pool]


def _const_spec(shape):
    nd = len(shape)
    return pl.BlockSpec(shape, lambda b, i: (0,) * nd, pipeline_mode=pl.Buffered(1))


def _token_mixing(x, mod, w_in, pool_w, pool_scale, sgu_g, sgu_b, sgu_w, sgu_bias, w_out, ln_g, ln_b, alpha):
    n_b, seq, d = x.shape
    d_sgu = sgu_g.shape[1]
    n_heads = sgu_w.shape[0]
    tm = TM_MIX
    return pl.pallas_call(
        functools.partial(_mix_kernel, alpha),
        grid=(n_b, seq // tm),
        in_specs=[
            pl.BlockSpec((1, tm, d), lambda b, i: (b, i, 0)),
            pl.BlockSpec((1, N_MOD, d), lambda b, i: (b, 0, 0)),
            _const_spec(w_in.shape), _const_spec(pool_w.shape), _const_spec(pool_scale.shape),
            _const_spec(sgu_g.shape), _const_spec(sgu_b.shape), _const_spec(sgu_w.shape),
            _const_spec(sgu_bias.shape), _const_spec(w_out.shape), _const_spec(ln_g.shape),
            _const_spec(ln_b.shape),
        ],
        out_specs=pl.BlockSpec((1, tm, d), lambda b, i: (b, i, 0)),
        out_shape=jax.ShapeDtypeStruct(x.shape, x.dtype),
        scratch_shapes=[
            pltpu.VMEM((POOL_HALO + tm, w_in.shape[1]), jnp.float32),
            pltpu.VMEM((tm, w_out.shape[1]), jnp.bfloat16),
            pltpu.VMEM((2, CHUNK, d_sgu), jnp.float32),
            pltpu.VMEM((n_heads // 2, CHUNK, 2 * CHUNK), jnp.bfloat16),
            pltpu.VMEM((n_heads // 2, 2 * CHUNK, 2 * CHUNK), jnp.bfloat16),
        ],
        compiler_params=pltpu.CompilerParams(
            dimension_semantics=("arbitrary", "arbitrary"),
            vmem_limit_bytes=VMEM_LIMIT_BYTES),
        name="token_mixing",
    )(x, mod, w_in, pool_w, pool_scale, sgu_g, sgu_b, sgu_w, sgu_bias, w_out, ln_g, ln_b)


def _shift_rows(g, prev, k):
    rolled = pltpu.roll(g, k, 0)
    head = lax.broadcasted_iota(jnp.int32, (SUBLANES, g.shape[1]), 0) < k
    top = jnp.where(head, pltpu.roll(prev, k, 0), rolled[:SUBLANES])
    return jnp.concatenate([top, rolled[SUBLANES:]], axis=0)


def _ffn_kernel(alpha, x_ref, mod_ref, w_up_ref, conv_w_ref, conv_b_ref, w_down_ref, ln_g_ref, ln_b_ref,
                o_ref, h_ref, carry_ref, acc_ref):
    i = pl.program_id(1)
    tm = x_ref.shape[1]
    n_chunks = w_up_ref.shape[0]
    fc = w_down_ref.shape[2]
    rb = FFN_ROW_BLOCK
    n_rb = tm // rb

    @pl.when(i == 0)
    def _():
        carry_ref[...] = jnp.zeros(carry_ref.shape, jnp.float32)

    shift = mod_ref[0, 3:4, :]
    scale = mod_ref[0, 4:5, :]
    gate = mod_ref[0, 5:6, :]

    def rows(r):
        return slice(r * rb, (r + 1) * rb)

    def modulate(r):
        h_ref[rows(r), :] = (x_ref[0, rows(r), :] * (1.0 + scale) + shift).astype(jnp.bfloat16)

    def up_proj(r, c):
        return jnp.dot(h_ref[rows(r), :], w_up_ref[c], preferred_element_type=jnp.float32)

    def activate(c, up):
        g = up[:, :fc]
        val = up[:, fc:]
        prev = carry_ref[c]
        carry_ref[c] = g[rb - SUBLANES:, :]
        cw = conv_w_ref[c]
        y = (conv_b_ref[c] + _shift_rows(g, prev, 2) * cw[0:1, :] + _shift_rows(g, prev, 1) * cw[1:2, :]
             + g * cw[2:3, :])
        return (_gelu(y) * val).astype(jnp.bfloat16)

    def down_proj(r, c, act):
        d = jnp.concatenate([jnp.dot(act, w_down_ref[c, j], preferred_element_type=jnp.float32)
                             for j in range(w_down_ref.shape[1])], axis=1)
        if c == 0:
            acc_ref[rows(r), :] = d
        else:
            acc_ref[rows(r), :] += d

    def finish(r):
        y = alpha * x_ref[0, rows(r), :] + gate * acc_ref[rows(r), :]
        o_ref[0, rows(r), :] = _layernorm(y, ln_g_ref[...], ln_b_ref[...])

    steps = [(r, c) for r in range(n_rb) for c in range(n_chunks)]
    modulate(0)
    pending = up_proj(*steps[0])
    for k, (r, c) in enumerate(steps):
        nxt = None
        if k + 1 < len(steps):
            r1, c1 = steps[k + 1]
            if c1 == 0:
                modulate(r1)
            nxt = up_proj(r1, c1)
        act = activate(c, pending)
        down_proj(r, c, act)
        if c == n_chunks - 1:
            finish(r)
        pending = nxt


def _channel_mixing(x, mod, w_up, conv_w, conv_b, w_down, ln_g, ln_b, alpha):
    n_b, seq, d = x.shape
    n_chunks, _, fc2 = w_up.shape
    fc = fc2 // 2
    tm = TM_FFN
    return pl.pallas_call(
        functools.partial(_ffn_kernel, alpha),
        grid=(n_b, seq // tm),
        in_specs=[
            pl.BlockSpec((1, tm, d), lambda b, i: (b, i, 0)),
            pl.BlockSpec((1, N_MOD, d), lambda b, i: (b, 0, 0)),
            _const_spec(w_up.shape), _const_spec(conv_w.shape), _const_spec(conv_b.shape),
            _const_spec(w_down.shape), _const_spec(ln_g.shape), _const_spec(ln_b.shape),
        ],
        out_specs=pl.BlockSpec((1, tm, d), lambda b, i: (b, i, 0)),
        out_shape=jax.ShapeDtypeStruct(x.shape, x.dtype),
        scratch_shapes=[
            pltpu.VMEM((tm, d), jnp.bfloat16),
            pltpu.VMEM((n_chunks, SUBLANES, fc), jnp.float32),
            pltpu.VMEM((tm, d), jnp.float32),
        ],
        compiler_params=pltpu.CompilerParams(
            dimension_semantics=("arbitrary", "arbitrary"),
            vmem_limit_bytes=VMEM_LIMIT_BYTES),
        name="channel_mixing",
    )(x, mod, w_up, conv_w, conv_b, w_down, ln_g, ln_b)


def kernel(x, c, ada_w, ada_b, w_in, pool_w, pool_scale, sgu_ln_g, sgu_ln_b, sgu_w, sgu_b, w_out, ln1_g, ln1_b,
           w_up, conv_w, conv_b, w_down, ln2_g, ln2_b):
    n_layers, d, _ = ada_w.shape
    n_b = x.shape[0]
    d_ff = w_down.shape[1]
    n_heads = sgu_w.shape[1]
    n_chunks = d_ff // FF_CHUNK
    alpha = (2.0 * n_layers) ** 0.25
    bf16 = jnp.bfloat16

    mod = _modulation(c, ada_w, ada_b).reshape(n_layers, n_b, N_MOD, d)

    w_in_b = w_in.astype(bf16)
    n_groups, gdim = pool_w.shape[1], pool_w.shape[2]
    pool_pairs = jnp.zeros((n_layers, n_groups // 2, 2 * gdim, 2 * gdim), pool_w.dtype)
    for g in range(n_groups):
        q = g % 2
        pool_pairs = pool_pairs.at[:, g // 2, q * gdim:(q + 1) * gdim, q * gdim:(q + 1) * gdim].set(pool_w[:, g])
    pool_w_b = pool_pairs.astype(bf16)
    d_mix = w_out.shape[1]
    w_out_b = jnp.transpose(w_out.reshape(n_layers, d_mix, N_HALVES, d // N_HALVES), (0, 2, 1, 3)).astype(bf16)
    w_up_g = w_up[:, :, :d_ff].reshape(n_layers, d, n_chunks, FF_CHUNK)
    w_up_v = w_up[:, :, d_ff:].reshape(n_layers, d, n_chunks, FF_CHUNK)
    w_up_b = jnp.transpose(jnp.concatenate([w_up_g, w_up_v], axis=-1), (0, 2, 1, 3)).astype(bf16)
    w_down_b = jnp.transpose(w_down.reshape(n_layers, n_chunks, FF_CHUNK, N_HALVES, d // N_HALVES),
                             (0, 1, 3, 2, 4)).astype(bf16)
    conv_w_c = jnp.transpose(conv_w.reshape(n_layers, CONV_WIDTH, n_chunks, FF_CHUNK), (0, 2, 1, 3))
    conv_b_c = conv_b.reshape(n_layers, n_chunks, 1, FF_CHUNK)
    hdim = sgu_ln_g.shape[1] // n_heads
    sgu_bias = jnp.broadcast_to(jnp.transpose(sgu_b, (0, 2, 1))[:, :, :, None],
                                (n_layers, CHUNK, n_heads, hdim)).reshape(n_layers, CHUNK, n_heads * hdim)

    row = lambda p, l: p[l].reshape(1, -1)
    for l in range(n_layers):
        x = _token_mixing(x, mod[l], w_in_b[l], pool_w_b[l], row(pool_scale, l), row(sgu_ln_g, l),
                          row(sgu_ln_b, l), sgu_w[l], sgu_bias[l], w_out_b[l], row(ln1_g, l), row(ln1_b, l), alpha)
        x = _channel_mixing(x, mod[l], w_up_b[l], conv_w_c[l], conv_b_c[l], w_down_b[l],
                            row(ln2_g, l), row(ln2_b, l), alpha)
    return x
```

```python
import functools
import math

import jax
import jax.numpy as jnp
from jax import lax
from jax.experimental import pallas as pl
from jax.experimental.pallas import tpu as pltpu

LANES = 128
SUBLANES = 8

POOL_WINDOWS = (2, 4, 8, 16)
POOL_HALO = 16
CHUNK = 128
CONV_WIDTH = 3
N_MOD = 6
LN_EPS = 1e-5
FF_CHUNK = 256
N_HALVES = 2

TM_MIX = 1024
TM_FFN = 512
FFN_ROW_BLOCK = 128
TN_MOD = 1536
VMEM_LIMIT_BYTES = 56 * 1024 * 1024

_GELU_C0 = math.sqrt(2.0 / math.pi)
_GELU_C1 = 0.044715 * _GELU_C0


def _gelu(x):
    inner = x * (_GELU_C0 + _GELU_C1 * (x * x))
    hx = 0.5 * x
    return hx + hx * jnp.tanh(inner)


def _layernorm(y, g, b):
    mu = jnp.mean(y, axis=-1, keepdims=True)
    yc = y - mu
    var = jnp.mean(yc * yc, axis=-1, keepdims=True)
    return yc * lax.rsqrt(var + LN_EPS) * g + b


def _mod_kernel(cb_ref, w_ref, b_ref, o_ref, ca_ref):
    n_b, d, _ = cb_ref.shape
    tn = w_ref.shape[2]

    @pl.when((pl.program_id(0) == 0) & (pl.program_id(1) == 0))
    def _():
        cb = cb_ref[...]
        ca_ref[...] = cb / (1.0 + jnp.exp(-cb))

    def row_group(rg, accs):
        start = pl.multiple_of(rg * SUBLANES, SUBLANES)
        w = w_ref[0, pl.ds(start, SUBLANES), :]
        return tuple(acc + w * jnp.tile(ca_ref[b, pl.ds(start, SUBLANES), :], (1, tn // LANES))
                     for b, acc in enumerate(accs))

    zero = jnp.zeros((SUBLANES, tn), jnp.float32)
    accs = lax.fori_loop(0, d // SUBLANES, row_group, (zero,) * n_b, unroll=8)
    for b in range(n_b):
        o_ref[0, b:b + 1, :] = jnp.sum(accs[b], axis=0, keepdims=True) + b_ref[0]


def _modulation(c, ada_w, ada_b):
    n_layers, d, n_out = ada_w.shape
    n_b = c.shape[0]
    cb = jnp.broadcast_to(c[:, :, None], (n_b, d, LANES))
    return pl.pallas_call(
        _mod_kernel,
        grid=(n_layers, n_out // TN_MOD),
        in_specs=[
            pl.BlockSpec((n_b, d, LANES), lambda l, j: (0, 0, 0)),
            pl.BlockSpec((1, d, TN_MOD), lambda l, j: (l, 0, j)),
            pl.BlockSpec((1, 1, TN_MOD), lambda l, j: (l, 0, j)),
        ],
        out_specs=pl.BlockSpec((1, n_b, TN_MOD), lambda l, j: (l, 0, j)),
        out_shape=jax.ShapeDtypeStruct((n_layers, n_b, n_out), jnp.float32),
        scratch_shapes=[pltpu.VMEM((n_b, d, LANES), jnp.float32)],
        compiler_params=pltpu.CompilerParams(
            dimension_semantics=("arbitrary", "arbitrary"),
            vmem_limit_bytes=VMEM_LIMIT_BYTES),
        name="adaln_modulation",
    )(cb, ada_w, ada_b.reshape(n_layers, 1, n_out))


def _mix_kernel(alpha, x_ref, mod_ref, w_in_ref, pool_w_ref, pool_scale_ref, sgu_g_ref, sgu_b_ref,
                sgu_w_ref, sgu_bias_ref, w_out_ref, ln_g_ref, ln_b_ref, o_ref,
                proj_ref, mix_ref, z_ref, wcat_ref, vdiag_ref):
    i = pl.program_id(1)
    tm = x_ref.shape[1]
    d_pool = pool_scale_ref.shape[1]
    d_sgu = sgu_g_ref.shape[1]
    gdim = d_pool // len(POOL_WINDOWS)
    n_heads = sgu_w_ref.shape[0]
    hdim = d_sgu // n_heads
    rb = CHUNK
    n_rb = tm // rb
    n_pairs = n_heads // 2

    @pl.when(i == 0)
    def _():
        proj_ref[0:POOL_HALO, :] = jnp.zeros((POOL_HALO, proj_ref.shape[1]), jnp.float32)
        tril = (lax.broadcasted_iota(jnp.int32, (CHUNK, CHUNK), 0)
                >= lax.broadcasted_iota(jnp.int32, (CHUNK, CHUNK), 1))
        for hd in range(n_heads):
            wcat_ref[hd // 2, :, (hd % 2) * CHUNK:(hd % 2 + 1) * CHUNK] = (
                jnp.where(tril, sgu_w_ref[hd], 0.0).astype(jnp.bfloat16))
        vdiag_ref[...] = jnp.zeros(vdiag_ref.shape, jnp.bfloat16)

    shift = mod_ref[0, 0:1, :]
    scale = mod_ref[0, 1:2, :]
    gate = mod_ref[0, 2:3, :]

    def rows(r):
        return slice(r * rb, (r + 1) * rb)

    def in_proj(r):
        h = (x_ref[0, rows(r), :] * (1.0 + scale) + shift).astype(jnp.bfloat16)
        proj_ref[POOL_HALO + r * rb:POOL_HALO + (r + 1) * rb, :] = jnp.dot(
            h, w_in_ref[...], preferred_element_type=jnp.float32)

    def mixers(r):
        pooled = []
        for g, w in enumerate(POOL_WINDOWS):
            s = proj_ref[r * rb:r * rb + POOL_HALO + rb, g * gdim:(g + 1) * gdim]
            a_g = s[POOL_HALO:]
            sh = 1
            while sh < w:
                s = s + pltpu.roll(s, sh, 0)
                sh *= 2
            s = s[POOL_HALO:]
            if r == 0:
                t = i * tm + lax.broadcasted_iota(jnp.int32, (POOL_HALO, gdim), 0)
                cnt = jnp.minimum(t + 1, w).astype(jnp.float32)
                mean = jnp.concatenate([s[:POOL_HALO] / cnt, s[POOL_HALO:] * (1.0 / w)], axis=0)
            else:
                mean = s * (1.0 / w)
            pooled.append((mean - a_g).astype(jnp.bfloat16))
        for p in range(len(POOL_WINDOWS) // 2):
            cols = slice(2 * p * gdim, 2 * (p + 1) * gdim)
            lhs = jnp.concatenate(pooled[2 * p:2 * p + 2], axis=1)
            mixed = jnp.dot(lhs, pool_w_ref[p], preferred_element_type=jnp.float32)
            mix_ref[rows(r), cols] = (mixed * pool_scale_ref[:, cols]).astype(jnp.bfloat16)
        v = _gelu(proj_ref[POOL_HALO + r * rb:POOL_HALO + (r + 1) * rb, d_pool + d_sgu:])
        vn = _layernorm(v, sgu_g_ref[...], sgu_b_ref[...]).astype(jnp.bfloat16)
        for p in range(n_pairs):
            for q in range(2):
                hd = 2 * p + q
                vdiag_ref[p, q * CHUNK:(q + 1) * CHUNK, q * hdim:(q + 1) * hdim] = vn[:, hd * hdim:(hd + 1) * hdim]
            z_ref[r % 2, :, 2 * p * hdim:2 * (p + 1) * hdim] = jnp.dot(
                wcat_ref[p], vdiag_ref[p], preferred_element_type=jnp.float32)

    def out_proj(r):
        u = _gelu(proj_ref[POOL_HALO + r * rb:POOL_HALO + (r + 1) * rb, d_pool:d_pool + d_sgu])
        mix_ref[rows(r), d_pool:] = (u * (z_ref[r % 2] + sgu_bias_ref[...])).astype(jnp.bfloat16)
        mix = mix_ref[rows(r), :]
        f = jnp.concatenate([jnp.dot(mix, w_out_ref[j], preferred_element_type=jnp.float32)
                             for j in range(w_out_ref.shape[0])], axis=1)
        y = alpha * x_ref[0, rows(r), :] + gate * f
        o_ref[0, rows(r), :] = _layernorm(y, ln_g_ref[...], ln_b_ref[...])

    in_proj(0)
    for k in range(n_rb + 1):
        if k + 1 < n_rb:
            in_proj(k + 1)
        if k < n_rb:
            mixers(k)
        if k >= 1:
            out_proj(k - 1)
    proj_ref[0:POOL_HALO, 0:d_pool] = proj_ref[tm:tm + POOL_HALO, 0:d_pool]


def _const_spec(shape):
    nd = len(shape)
    return pl.BlockSpec(shape, lambda b, i: (0,) * nd, pipeline_mode=pl.Buffered(1))


def _token_mixing(x, mod, w_in, pool_w, pool_scale, sgu_g, sgu_b, sgu_w, sgu_bias, w_out, ln_g, ln_b, alpha):
    n_b, seq, d = x.shape
    d_sgu = sgu_g.shape[1]
    n_heads = sgu_w.shape[0]
    tm = TM_MIX
    return pl.pallas_call(
        functools.partial(_mix_kernel, alpha),
        grid=(n_b, seq // tm),
        in_specs=[
            pl.BlockSpec((1, tm, d), lambda b, i: (b, i, 0)),
            pl.BlockSpec((1, N_MOD, d), lambda b, i: (b, 0, 0)),
            _const_spec(w_in.shape), _const_spec(pool_w.shape), _const_spec(pool_scale.shape),
            _const_spec(sgu_g.shape), _const_spec(sgu_b.shape), _const_spec(sgu_w.shape),
            _const_spec(sgu_bias.shape), _const_spec(w_out.shape), _const_spec(ln_g.shape),
            _const_spec(ln_b.shape),
        ],
        out_specs=pl.BlockSpec((1, tm, d), lambda b, i: (b, i, 0)),
        out_shape=jax.ShapeDtypeStruct(x.shape, x.dtype),
        scratch_shapes=[
            pltpu.VMEM((POOL_HALO + tm, w_in.shape[1]), jnp.float32),
            pltpu.VMEM((tm, w_out.shape[1]), jnp.bfloat16),
            pltpu.VMEM((2, CHUNK, d_sgu), jnp.float32),
            pltpu.VMEM((n_heads // 2, CHUNK, 2 * CHUNK), jnp.bfloat16),
            pltpu.VMEM((n_heads // 2, 2 * CHUNK, 2 * CHUNK), jnp.bfloat16),
        ],
        compiler_params=pltpu.CompilerParams(
            dimension_semantics=("arbitrary", "arbitrary"),
            vmem_limit_bytes=VMEM_LIMIT_BYTES),
        name="token_mixing",
    )(x, mod, w_in, pool_w, pool_scale, sgu_g, sgu_b, sgu_w, sgu_bias, w_out, ln_g, ln_b)


def _shift_rows(g, prev, k):
    rolled = pltpu.roll(g, k, 0)
    head = lax.broadcasted_iota(jnp.int32, (SUBLANES, g.shape[1]), 0) < k
    top = jnp.where(head, pltpu.roll(prev, k, 0), rolled[:SUBLANES])
    return jnp.concatenate([top, rolled[SUBLANES:]], axis=0)


def _ffn_kernel(alpha, x_ref, mod_ref, w_up_ref, conv_w_ref, conv_b_ref, w_down_ref, ln_g_ref, ln_b_ref,
                o_ref, h_ref, carry_ref, acc_ref):
    i = pl.program_id(1)
    tm = x_ref.shape[1]
    n_chunks, fc, _ = w_down_ref.shape
    d_ff = n_chunks * fc
    rb = FFN_ROW_BLOCK
    n_rb = tm // rb

    @pl.when(i == 0)
    def _():
        carry_ref[...] = jnp.zeros(carry_ref.shape, jnp.float32)

    shift = mod_ref[0, 3:4, :]
    scale = mod_ref[0, 4:5, :]
    gate = mod_ref[0, 5:6, :]

    def rows(r):
        return slice(r * rb, (r + 1) * rb)

    def modulate(r):
        h_ref[rows(r), :] = (x_ref[0, rows(r), :] * (1.0 + scale) + shift).astype(jnp.bfloat16)

    def up_proj(r, c):
        h = h_ref[rows(r), :]
        g = jnp.dot(h, w_up_ref[:, c * fc:(c + 1) * fc], preferred_element_type=jnp.float32)
        val = jnp.dot(h, w_up_ref[:, d_ff + c * fc:d_ff + (c + 1) * fc], preferred_element_type=jnp.float32)
        return g, val

    def activate(c, up):
        g, val = up
        prev = carry_ref[c]
        carry_ref[c] = g[rb - SUBLANES:, :]
        cw = conv_w_ref[c]
        y = (conv_b_ref[c] + _shift_rows(g, prev, 2) * cw[0:1, :] + _shift_rows(g, prev, 1) * cw[1:2, :]
             + g * cw[2:3, :])
        return (_gelu(y) * val).astype(jnp.bfloat16)

    def down_proj(r, c, act):
        d = jnp.dot(act, w_down_ref[c], preferred_element_type=jnp.float32)
        if c == 0:
            acc_ref[rows(r), :] = d
        else:
            acc_ref[rows(r), :] += d

    def finish(r):
        y = alpha * x_ref[0, rows(r), :] + gate * acc_ref[rows(r), :]
        o_ref[0, rows(r), :] = _layernorm(y, ln_g_ref[...], ln_b_ref[...])

    steps = [(r, c) for r in range(n_rb) for c in range(n_chunks)]
    modulate(0)
    pending = up_proj(*steps[0])
    for k, (r, c) in enumerate(steps):
        nxt = None
        if k + 1 < len(steps):
            r1, c1 = steps[k + 1]
            if c1 == 0:
                modulate(r1)
            nxt = up_proj(r1, c1)
        act = activate(c, pending)
        down_proj(r, c, act)
        if c == n_chunks - 1:
            finish(r)
        pending = nxt


def _channel_mixing(x, mod, w_up, conv_w, conv_b, w_down, ln_g, ln_b, alpha):
    n_b, seq, d = x.shape
    n_chunks, fc, _ = w_down.shape
    tm = TM_FFN
    return pl.pallas_call(
        functools.partial(_ffn_kernel, alpha),
        grid=(n_b, seq // tm),
        in_specs=[
            pl.BlockSpec((1, tm, d), lambda b, i: (b, i, 0)),
            pl.BlockSpec((1, N_MOD, d), lambda b, i: (b, 0, 0)),
            _const_spec(w_up.shape), _const_spec(conv_w.shape), _const_spec(conv_b.shape),
            _const_spec(w_down.shape), _const_spec(ln_g.shape), _const_spec(ln_b.shape),
        ],
        out_specs=pl.BlockSpec((1, tm, d), lambda b, i: (b, i, 0)),
        out_shape=jax.ShapeDtypeStruct(x.shape, x.dtype),
        scratch_shapes=[
            pltpu.VMEM((tm, d), jnp.bfloat16),
            pltpu.VMEM((n_chunks, SUBLANES, fc), jnp.float32),
            pltpu.VMEM((tm, d), jnp.float32),
        ],
        compiler_params=pltpu.CompilerParams(
            dimension_semantics=("arbitrary", "arbitrary"),
            vmem_limit_bytes=VMEM_LIMIT_BYTES),
        name="channel_mixing",
    )(x, mod, w_up, conv_w, conv_b, w_down, ln_g, ln_b)


def kernel(x, c, ada_w, ada_b, w_in, pool_w, pool_scale, sgu_ln_g, sgu_ln_b, sgu_w, sgu_b, w_out, ln1_g, ln1_b,
           w_up, conv_w, conv_b, w_down, ln2_g, ln2_b):
    n_layers, d, _ = ada_w.shape
    n_b = x.shape[0]
    d_ff = w_down.shape[1]
    n_heads = sgu_w.shape[1]
    n_chunks = d_ff // FF_CHUNK
    alpha = (2.0 * n_layers) ** 0.25
    bf16 = jnp.bfloat16

    mod = _modulation(c, ada_w, ada_b).reshape(n_layers, n_b, N_MOD, d)

    w_in_b = w_in.astype(bf16)
    n_groups, gdim = pool_w.shape[1], pool_w.shape[2]
    pool_pairs = jnp.zeros((n_layers, n_groups // 2, 2 * gdim, 2 * gdim), pool_w.dtype)
    for g in range(n_groups):
        q = g % 2
        pool_pairs = pool_pairs.at[:, g // 2, q * gdim:(q + 1) * gdim, q * gdim:(q + 1) * gdim].set(pool_w[:, g])
    pool_w_b = pool_pairs.astype(bf16)
    d_mix = w_out.shape[1]
    w_out_b = jnp.transpose(w_out.reshape(n_layers, d_mix, N_HALVES, d // N_HALVES), (0, 2, 1, 3)).astype(bf16)
    w_up_b = w_up.astype(bf16)
    w_down_b = w_down.reshape(n_layers, n_chunks, FF_CHUNK, d).astype(bf16)
    conv_w_c = jnp.transpose(conv_w.reshape(n_layers, CONV_WIDTH, n_chunks, FF_CHUNK), (0, 2, 1, 3))
    conv_b_c = conv_b.reshape(n_layers, n_chunks, 1, FF_CHUNK)
    hdim = sgu_ln_g.shape[1] // n_heads
    sgu_bias = jnp.broadcast_to(jnp.transpose(sgu_b, (0, 2, 1))[:, :, :, None],
                                (n_layers, CHUNK, n_heads, hdim)).reshape(n_layers, CHUNK, n_heads * hdim)

    row = lambda p, l: p[l].reshape(1, -1)
    for l in range(n_layers):
        x = _token_mixing(x, mod[l], w_in_b[l], pool_w_b[l], row(pool_scale, l), row(sgu_ln_g, l),
                          row(sgu_ln_b, l), sgu_w[l], sgu_bias[l], w_out_b[l], row(ln1_g, l), row(ln1_b, l), alpha)
        x = _channel_mixing(x, mod[l], w_up_b[l], conv_w_c[l], conv_b_c[l], w_down_b[l],
                            row(ln2_g, l), row(ln2_b, l), alpha)
    return x
```

```python
import functools
import math

import jax
import jax.numpy as jnp
from jax import lax
from jax.experimental import pallas as pl
from jax.experimental.pallas import tpu as pltpu

LANES = 128
SUBLANES = 8

POOL_WINDOWS = (2, 4, 8, 16)
POOL_HALO = 16
CHUNK = 128
CONV_WIDTH = 3
N_MOD = 6
LN_EPS = 1e-5
FF_CHUNK = 256
N_HALVES = 2

TM_MIX = 1024
TM_FFN = 1024
FFN_ROW_BLOCK = 128
TN_MOD = 1536
VMEM_LIMIT_BYTES = 56 * 1024 * 1024

_GELU_C0 = math.sqrt(2.0 / math.pi)
_GELU_C1 = 0.044715 * _GELU_C0


def _gelu(x):
    inner = x * (_GELU_C0 + _GELU_C1 * (x * x))
    hx = 0.5 * x
    return hx + hx * jnp.tanh(inner)


def _layernorm(y, g, b):
    mu = jnp.mean(y, axis=-1, keepdims=True)
    yc = y - mu
    var = jnp.mean(yc * yc, axis=-1, keepdims=True)
    return yc * lax.rsqrt(var + LN_EPS) * g + b


def _mod_kernel(cb_ref, w_ref, b_ref, o_ref, ca_ref):
    n_b, d, _ = cb_ref.shape
    tn = w_ref.shape[2]

    @pl.when((pl.program_id(0) == 0) & (pl.program_id(1) == 0))
    def _():
        cb = cb_ref[...]
        ca_ref[...] = cb / (1.0 + jnp.exp(-cb))

    def row_group(rg, accs):
        start = pl.multiple_of(rg * SUBLANES, SUBLANES)
        w = w_ref[0, pl.ds(start, SUBLANES), :]
        return tuple(acc + w * jnp.tile(ca_ref[b, pl.ds(start, SUBLANES), :], (1, tn // LANES))
                     for b, acc in enumerate(accs))

    zero = jnp.zeros((SUBLANES, tn), jnp.float32)
    accs = lax.fori_loop(0, d // SUBLANES, row_group, (zero,) * n_b, unroll=8)
    for b in range(n_b):
        o_ref[0, b:b + 1, :] = jnp.sum(accs[b], axis=0, keepdims=True) + b_ref[0]


def _modulation(c, ada_w, ada_b):
    n_layers, d, n_out = ada_w.shape
    n_b = c.shape[0]
    cb = jnp.broadcast_to(c[:, :, None], (n_b, d, LANES))
    return pl.pallas_call(
        _mod_kernel,
        grid=(n_layers, n_out // TN_MOD),
        in_specs=[
            pl.BlockSpec((n_b, d, LANES), lambda l, j: (0, 0, 0)),
            pl.BlockSpec((1, d, TN_MOD), lambda l, j: (l, 0, j)),
            pl.BlockSpec((1, 1, TN_MOD), lambda l, j: (l, 0, j)),
        ],
        out_specs=pl.BlockSpec((1, n_b, TN_MOD), lambda l, j: (l, 0, j)),
        out_shape=jax.ShapeDtypeStruct((n_layers, n_b, n_out), jnp.float32),
        scratch_shapes=[pltpu.VMEM((n_b, d, LANES), jnp.float32)],
        compiler_params=pltpu.CompilerParams(
            dimension_semantics=("arbitrary", "arbitrary"),
            vmem_limit_bytes=VMEM_LIMIT_BYTES),
        name="adaln_modulation",
    )(cb, ada_w, ada_b.reshape(n_layers, 1, n_out))


def _mix_kernel(alpha, x_ref, mod_ref, w_in_ref, pool_w_ref, pool_scale_ref, sgu_g_ref, sgu_b_ref,
                sgu_w_ref, sgu_bias_ref, w_out_ref, ln_g_ref, ln_b_ref, o_ref,
                proj_ref, mix_ref, z_ref, wcat_ref, vdiag_ref):
    i = pl.program_id(1)
    tm = x_ref.shape[1]
    d_pool = pool_scale_ref.shape[1]
    d_sgu = sgu_g_ref.shape[1]
    gdim = d_pool // len(POOL_WINDOWS)
    n_heads = sgu_w_ref.shape[0]
    hdim = d_sgu // n_heads
    rb = CHUNK
    n_rb = tm // rb
    n_pairs = n_heads // 2

    @pl.when(i == 0)
    def _():
        proj_ref[0:POOL_HALO, :] = jnp.zeros((POOL_HALO, proj_ref.shape[1]), jnp.float32)
        tril = (lax.broadcasted_iota(jnp.int32, (CHUNK, CHUNK), 0)
                >= lax.broadcasted_iota(jnp.int32, (CHUNK, CHUNK), 1))
        for hd in range(n_heads):
            wcat_ref[hd // 2, :, (hd % 2) * CHUNK:(hd % 2 + 1) * CHUNK] = (
                jnp.where(tril, sgu_w_ref[hd], 0.0).astype(jnp.bfloat16))
        vdiag_ref[...] = jnp.zeros(vdiag_ref.shape, jnp.bfloat16)

    shift = mod_ref[0, 0:1, :]
    scale = mod_ref[0, 1:2, :]
    gate = mod_ref[0, 2:3, :]

    def rows(r):
        return slice(r * rb, (r + 1) * rb)

    def in_proj(r):
        h = (x_ref[0, rows(r), :] * (1.0 + scale) + shift).astype(jnp.bfloat16)
        proj_ref[POOL_HALO + r * rb:POOL_HALO + (r + 1) * rb, :] = jnp.dot(
            h, w_in_ref[...], preferred_element_type=jnp.float32)

    def mixers(r):
        pooled = []
        for g, w in enumerate(POOL_WINDOWS):
            s = proj_ref[r * rb:r * rb + POOL_HALO + rb, g * gdim:(g + 1) * gdim]
            a_g = s[POOL_HALO:]
            sh = 1
            while sh < w:
                s = s + pltpu.roll(s, sh, 0)
                sh *= 2
            s = s[POOL_HALO:]
            if r == 0:
                t = i * tm + lax.broadcasted_iota(jnp.int32, (POOL_HALO, gdim), 0)
                cnt = jnp.minimum(t + 1, w).astype(jnp.float32)
                mean = jnp.concatenate([s[:POOL_HALO] / cnt, s[POOL_HALO:] * (1.0 / w)], axis=0)
            else:
                mean = s * (1.0 / w)
            pooled.append((mean - a_g).astype(jnp.bfloat16))
        for p in range(len(POOL_WINDOWS) // 2):
            cols = slice(2 * p * gdim, 2 * (p + 1) * gdim)
            lhs = jnp.concatenate(pooled[2 * p:2 * p + 2], axis=1)
            mixed = jnp.dot(lhs, pool_w_ref[p], preferred_element_type=jnp.float32)
            mix_ref[rows(r), cols] = (mixed * pool_scale_ref[:, cols]).astype(jnp.bfloat16)
        v = _gelu(proj_ref[POOL_HALO + r * rb:POOL_HALO + (r + 1) * rb, d_pool + d_sgu:])
        vn = _layernorm(v, sgu_g_ref[...], sgu_b_ref[...]).astype(jnp.bfloat16)
        for p in range(n_pairs):
            for q in range(2):
                hd = 2 * p + q
                vdiag_ref[p, q * CHUNK:(q + 1) * CHUNK, q * hdim:(q + 1) * hdim] = vn[:, hd * hdim:(hd + 1) * hdim]
            z_ref[r % 2, :, 2 * p * hdim:2 * (p + 1) * hdim] = jnp.dot(
                wcat_ref[p], vdiag_ref[p], preferred_element_type=jnp.float32)

    def out_proj(r):
        u = _gelu(proj_ref[POOL_HALO + r * rb:POOL_HALO + (r + 1) * rb, d_pool:d_pool + d_sgu])
        mix_ref[rows(r), d_pool:] = (u * (z_ref[r % 2] + sgu_bias_ref[...])).astype(jnp.bfloat16)
        mix = mix_ref[rows(r), :]
        f = jnp.concatenate([jnp.dot(mix, w_out_ref[j], preferred_element_type=jnp.float32)
                             for j in range(w_out_ref.shape[0])], axis=1)
        y = alpha * x_ref[0, rows(r), :] + gate * f
        o_ref[0, rows(r), :] = _layernorm(y, ln_g_ref[...], ln_b_ref[...])

    in_proj(0)
    for k in range(n_rb + 1):
        if k + 1 < n_rb:
            in_proj(k + 1)
        if k < n_rb:
            mixers(k)
        if k >= 1:
            out_proj(k - 1)
    proj_ref[0:POOL_HALO, 0:d_pool] = proj_ref[tm:tm + POOL_HALO, 0:d_pool]


def _layer_spec(stacked, l):
    shape = stacked.shape[1:]
    return pl.BlockSpec((None,) + shape, lambda b, i: (l,) + (0,) * len(shape), pipeline_mode=pl.Buffered(1))


def _mod_spec(mod, l):
    return pl.BlockSpec((None, 1) + mod.shape[2:], lambda b, i: (l, b, 0, 0))


def _token_mixing(x, l, mod, w_in, pool_w, pool_scale, sgu_g, sgu_b, sgu_w, sgu_bias, w_out, ln_g, ln_b, alpha):
    n_b, seq, d = x.shape
    d_sgu = sgu_g.shape[-1]
    n_heads = sgu_w.shape[1]
    tm = TM_MIX
    return pl.pallas_call(
        functools.partial(_mix_kernel, alpha),
        grid=(n_b, seq // tm),
        in_specs=[
            pl.BlockSpec((1, tm, d), lambda b, i: (b, i, 0)),
            _mod_spec(mod, l),
        ] + [_layer_spec(p, l) for p in (w_in, pool_w, pool_scale, sgu_g, sgu_b, sgu_w, sgu_bias, w_out, ln_g, ln_b)],
        out_specs=pl.BlockSpec((1, tm, d), lambda b, i: (b, i, 0)),
        out_shape=jax.ShapeDtypeStruct(x.shape, x.dtype),
        scratch_shapes=[
            pltpu.VMEM((POOL_HALO + tm, w_in.shape[-1]), jnp.float32),
            pltpu.VMEM((tm, w_out.shape[-2]), jnp.bfloat16),
            pltpu.VMEM((2, CHUNK, d_sgu), jnp.float32),
            pltpu.VMEM((n_heads // 2, CHUNK, 2 * CHUNK), jnp.bfloat16),
            pltpu.VMEM((n_heads // 2, 2 * CHUNK, 2 * CHUNK), jnp.bfloat16),
        ],
        compiler_params=pltpu.CompilerParams(
            dimension_semantics=("arbitrary", "arbitrary"),
            vmem_limit_bytes=VMEM_LIMIT_BYTES),
        name="token_mixing",
    )(x, mod, w_in, pool_w, pool_scale, sgu_g, sgu_b, sgu_w, sgu_bias, w_out, ln_g, ln_b)


def _shift_rows(g, prev, k):
    rolled = pltpu.roll(g, k, 0)
    head = lax.broadcasted_iota(jnp.int32, (SUBLANES, g.shape[1]), 0) < k
    top = jnp.where(head, pltpu.roll(prev, k, 0), rolled[:SUBLANES])
    return jnp.concatenate([top, rolled[SUBLANES:]], axis=0)


def _ffn_kernel(alpha, x_ref, mod_ref, w_up_ref, conv_w_ref, conv_b_ref, w_down_ref, ln_g_ref, ln_b_ref,
                o_ref, h_ref, carry_ref, acc_ref):
    i = pl.program_id(1)
    tm = x_ref.shape[1]
    n_chunks, fc, _ = w_down_ref.shape
    d_ff = n_chunks * fc
    rb = FFN_ROW_BLOCK
    n_rb = tm // rb

    @pl.when(i == 0)
    def _():
        carry_ref[...] = jnp.zeros(carry_ref.shape, jnp.float32)

    shift = mod_ref[0, 3:4, :]
    scale = mod_ref[0, 4:5, :]
    gate = mod_ref[0, 5:6, :]

    def rows(r):
        return slice(r * rb, (r + 1) * rb)

    def modulate(r):
        h_ref[rows(r), :] = (x_ref[0, rows(r), :] * (1.0 + scale) + shift).astype(jnp.bfloat16)

    def up_proj(r, c):
        h = h_ref[rows(r), :]
        g = jnp.dot(h, w_up_ref[:, c * fc:(c + 1) * fc], preferred_element_type=jnp.float32)
        val = jnp.dot(h, w_up_ref[:, d_ff + c * fc:d_ff + (c + 1) * fc], preferred_element_type=jnp.float32)
        return g, val

    def activate(c, up):
        g, val = up
        prev = carry_ref[c]
        carry_ref[c] = g[rb - SUBLANES:, :]
        cw = conv_w_ref[c]
        y = (conv_b_ref[c] + _shift_rows(g, prev, 2) * cw[0:1, :] + _shift_rows(g, prev, 1) * cw[1:2, :]
             + g * cw[2:3, :])
        return (_gelu(y) * val).astype(jnp.bfloat16)

    def down_proj(r, c, act):
        d = jnp.dot(act, w_down_ref[c], preferred_element_type=jnp.float32)
        if c == 0:
            acc_ref[rows(r), :] = d
        else:
            acc_ref[rows(r), :] += d

    def finish(r):
        y = alpha * x_ref[0, rows(r), :] + gate * acc_ref[rows(r), :]
        o_ref[0, rows(r), :] = _layernorm(y, ln_g_ref[...], ln_b_ref[...])

    steps = [(r, c) for r in range(n_rb) for c in range(n_chunks)]
    modulate(0)
    pending = up_proj(*steps[0])
    for k, (r, c) in enumerate(steps):
        nxt = None
        if k + 1 < len(steps):
            r1, c1 = steps[k + 1]
            if c1 == 0:
                modulate(r1)
            nxt = up_proj(r1, c1)
        act = activate(c, pending)
        down_proj(r, c, act)
        if c == n_chunks - 1:
            finish(r)
        pending = nxt


def _channel_mixing(x, l, mod, w_up, conv_w, conv_b, w_down, ln_g, ln_b, alpha):
    n_b, seq, d = x.shape
    _, n_chunks, fc, _ = w_down.shape
    tm = TM_FFN
    return pl.pallas_call(
        functools.partial(_ffn_kernel, alpha),
        grid=(n_b, seq // tm),
        in_specs=[
            pl.BlockSpec((1, tm, d), lambda b, i: (b, i, 0)),
            _mod_spec(mod, l),
        ] + [_layer_spec(p, l) for p in (w_up, conv_w, conv_b, w_down, ln_g, ln_b)],
        out_specs=pl.BlockSpec((1, tm, d), lambda b, i: (b, i, 0)),
        out_shape=jax.ShapeDtypeStruct(x.shape, x.dtype),
        scratch_shapes=[
            pltpu.VMEM((tm, d), jnp.bfloat16),
            pltpu.VMEM((n_chunks, SUBLANES, fc), jnp.float32),
            pltpu.VMEM((tm, d), jnp.float32),
        ],
        compiler_params=pltpu.CompilerParams(
            dimension_semantics=("arbitrary", "arbitrary"),
            vmem_limit_bytes=VMEM_LIMIT_BYTES),
        name="channel_mixing",
    )(x, mod, w_up, conv_w, conv_b, w_down, ln_g, ln_b)


def kernel(x, c, ada_w, ada_b, w_in, pool_w, pool_scale, sgu_ln_g, sgu_ln_b, sgu_w, sgu_b, w_out, ln1_g, ln1_b,
           w_up, conv_w, conv_b, w_down, ln2_g, ln2_b):
    n_layers, d, _ = ada_w.shape
    n_b = x.shape[0]
    d_ff = w_down.shape[1]
    n_heads = sgu_w.shape[1]
    n_chunks = d_ff // FF_CHUNK
    alpha = (2.0 * n_layers) ** 0.25
    bf16 = jnp.bfloat16

    mod = _modulation(c, ada_w, ada_b).reshape(n_layers, n_b, N_MOD, d)

    w_in_b = w_in.astype(bf16)
    n_groups, gdim = pool_w.shape[1], pool_w.shape[2]
    pool_pairs = jnp.zeros((n_layers, n_groups // 2, 2 * gdim, 2 * gdim), pool_w.dtype)
    for g in range(n_groups):
        q = g % 2
        pool_pairs = pool_pairs.at[:, g // 2, q * gdim:(q + 1) * gdim, q * gdim:(q + 1) * gdim].set(pool_w[:, g])
    pool_w_b = pool_pairs.astype(bf16)
    d_mix = w_out.shape[1]
    w_out_b = jnp.transpose(w_out.reshape(n_layers, d_mix, N_HALVES, d // N_HALVES), (0, 2, 1, 3)).astype(bf16)
    w_up_b = w_up.astype(bf16)
    w_down_b = w_down.reshape(n_layers, n_chunks, FF_CHUNK, d).astype(bf16)
    conv_w_c = jnp.transpose(conv_w.reshape(n_layers, CONV_WIDTH, n_chunks, FF_CHUNK), (0, 2, 1, 3))
    conv_b_c = conv_b.reshape(n_layers, n_chunks, 1, FF_CHUNK)
    hdim = sgu_ln_g.shape[1] // n_heads
    sgu_bias = jnp.broadcast_to(jnp.transpose(sgu_b, (0, 2, 1))[:, :, :, None],
                                (n_layers, CHUNK, n_heads, hdim)).reshape(n_layers, CHUNK, n_heads * hdim)

    rows = lambda p: p.reshape(n_layers, 1, -1)
    for l in range(n_layers):
        x = _token_mixing(x, l, mod, w_in_b, pool_w_b, rows(pool_scale), rows(sgu_ln_g), rows(sgu_ln_b), sgu_w,
                          sgu_bias, w_out_b, rows(ln1_g), rows(ln1_b), alpha)
        x = _channel_mixing(x, l, mod, w_up_b, conv_w_c, conv_b_c, w_down_b, rows(ln2_g), rows(ln2_b), alpha)
    return x
```

```python
import functools
import math

import jax
import jax.numpy as jnp
from jax import lax
from jax.experimental import pallas as pl
from jax.experimental.pallas import tpu as pltpu

LANES = 128
SUBLANES = 8

POOL_WINDOWS = (2, 4, 8, 16)
POOL_HALO = 16
CHUNK = 128
CONV_WIDTH = 3
N_MOD = 6
LN_EPS = 1e-5
FF_CHUNK = 256
N_HALVES = 2

TM_MIX = 2048
RING = 4
TM_FFN = 1024
FFN_ROW_BLOCK = 128
TN_MOD = 1536
VMEM_LIMIT_BYTES = 56 * 1024 * 1024

_GELU_C0 = math.sqrt(2.0 / math.pi)
_GELU_C1 = 0.044715 * _GELU_C0


def _gelu(x):
    inner = x * (_GELU_C0 + _GELU_C1 * (x * x))
    hx = 0.5 * x
    return hx + hx * jnp.tanh(inner)


def _layernorm(y, g, b):
    mu = jnp.mean(y, axis=-1, keepdims=True)
    yc = y - mu
    var = jnp.mean(yc * yc, axis=-1, keepdims=True)
    return yc * lax.rsqrt(var + LN_EPS) * g + b


def _mod_kernel(cb_ref, w_ref, b_ref, o_ref, ca_ref):
    n_b, d, _ = cb_ref.shape
    tn = w_ref.shape[2]

    @pl.when((pl.program_id(0) == 0) & (pl.program_id(1) == 0))
    def _():
        cb = cb_ref[...]
        ca_ref[...] = cb / (1.0 + jnp.exp(-cb))

    def row_group(rg, accs):
        start = pl.multiple_of(rg * SUBLANES, SUBLANES)
        w = w_ref[0, pl.ds(start, SUBLANES), :]
        return tuple(acc + w * jnp.tile(ca_ref[b, pl.ds(start, SUBLANES), :], (1, tn // LANES))
                     for b, acc in enumerate(accs))

    zero = jnp.zeros((SUBLANES, tn), jnp.float32)
    accs = lax.fori_loop(0, d // SUBLANES, row_group, (zero,) * n_b, unroll=8)
    for b in range(n_b):
        o_ref[0, b:b + 1, :] = jnp.sum(accs[b], axis=0, keepdims=True) + b_ref[0]


def _modulation(c, ada_w, ada_b):
    n_layers, d, n_out = ada_w.shape
    n_b = c.shape[0]
    cb = jnp.broadcast_to(c[:, :, None], (n_b, d, LANES))
    return pl.pallas_call(
        _mod_kernel,
        grid=(n_layers, n_out // TN_MOD),
        in_specs=[
            pl.BlockSpec((n_b, d, LANES), lambda l, j: (0, 0, 0)),
            pl.BlockSpec((1, d, TN_MOD), lambda l, j: (l, 0, j)),
            pl.BlockSpec((1, 1, TN_MOD), lambda l, j: (l, 0, j)),
        ],
        out_specs=pl.BlockSpec((1, n_b, TN_MOD), lambda l, j: (l, 0, j)),
        out_shape=jax.ShapeDtypeStruct((n_layers, n_b, n_out), jnp.float32),
        scratch_shapes=[pltpu.VMEM((n_b, d, LANES), jnp.float32)],
        compiler_params=pltpu.CompilerParams(
            dimension_semantics=("arbitrary", "arbitrary"),
            vmem_limit_bytes=VMEM_LIMIT_BYTES),
        name="adaln_modulation",
    )(cb, ada_w, ada_b.reshape(n_layers, 1, n_out))


def _mix_kernel(alpha, x_ref, mod_ref, w_in_ref, pool_w_ref, pool_scale_ref, sgu_g_ref, sgu_b_ref,
                sgu_w_ref, sgu_bias_ref, w_out_ref, ln_g_ref, ln_b_ref, o_ref,
                a_ref, uv_ref, mix_ref, z_ref, wcat_ref, vdiag_ref):
    i = pl.program_id(1)
    tm = x_ref.shape[1]
    d_pool = pool_scale_ref.shape[1]
    d_sgu = sgu_g_ref.shape[1]
    gdim = d_pool // len(POOL_WINDOWS)
    n_heads = sgu_w_ref.shape[0]
    hdim = d_sgu // n_heads
    rb = CHUNK
    n_rb = tm // rb
    n_pairs = n_heads // 2

    @pl.when(i == 0)
    def _():
        a_ref[0:POOL_HALO, :] = jnp.zeros((POOL_HALO, d_pool), jnp.float32)
        tril = (lax.broadcasted_iota(jnp.int32, (CHUNK, CHUNK), 0)
                >= lax.broadcasted_iota(jnp.int32, (CHUNK, CHUNK), 1))
        for hd in range(n_heads):
            wcat_ref[hd // 2, :, (hd % 2) * CHUNK:(hd % 2 + 1) * CHUNK] = (
                jnp.where(tril, sgu_w_ref[hd], 0.0).astype(jnp.bfloat16))
        vdiag_ref[...] = jnp.zeros(vdiag_ref.shape, jnp.bfloat16)

    shift = mod_ref[0, 0:1, :]
    scale = mod_ref[0, 1:2, :]
    gate = mod_ref[0, 2:3, :]

    def rows(r):
        return slice(r * rb, (r + 1) * rb)

    def in_proj(r):
        h = (x_ref[0, rows(r), :] * (1.0 + scale) + shift).astype(jnp.bfloat16)
        proj = jnp.dot(h, w_in_ref[...], preferred_element_type=jnp.float32)
        a_ref[POOL_HALO + r * rb:POOL_HALO + (r + 1) * rb, :] = proj[:, :d_pool]
        uv_ref[r % RING] = proj[:, d_pool:]

    def mixers(r):
        pooled = []
        for g, w in enumerate(POOL_WINDOWS):
            s = a_ref[r * rb:r * rb + POOL_HALO + rb, g * gdim:(g + 1) * gdim]
            a_g = s[POOL_HALO:]
            sh = 1
            while sh < w:
                s = s + pltpu.roll(s, sh, 0)
                sh *= 2
            s = s[POOL_HALO:]
            if r == 0:
                t = i * tm + lax.broadcasted_iota(jnp.int32, (POOL_HALO, gdim), 0)
                cnt = jnp.minimum(t + 1, w).astype(jnp.float32)
                mean = jnp.concatenate([s[:POOL_HALO] / cnt, s[POOL_HALO:] * (1.0 / w)], axis=0)
            else:
                mean = s * (1.0 / w)
            pooled.append((mean - a_g).astype(jnp.bfloat16))
        for p in range(len(POOL_WINDOWS) // 2):
            cols = slice(2 * p * gdim, 2 * (p + 1) * gdim)
            lhs = jnp.concatenate(pooled[2 * p:2 * p + 2], axis=1)
            mixed = jnp.dot(lhs, pool_w_ref[p], preferred_element_type=jnp.float32)
            mix_ref[r % RING, :, cols] = (mixed * pool_scale_ref[:, cols]).astype(jnp.bfloat16)
        v = _gelu(uv_ref[r % RING, :, d_sgu:])
        vn = _layernorm(v, sgu_g_ref[...], sgu_b_ref[...]).astype(jnp.bfloat16)
        for p in range(n_pairs):
            for q in range(2):
                hd = 2 * p + q
                vdiag_ref[p, q * CHUNK:(q + 1) * CHUNK, q * hdim:(q + 1) * hdim] = vn[:, hd * hdim:(hd + 1) * hdim]
            z_ref[r % 2, :, 2 * p * hdim:2 * (p + 1) * hdim] = jnp.dot(
                wcat_ref[p], vdiag_ref[p], preferred_element_type=jnp.float32)

    def out_proj(r):
        u = _gelu(uv_ref[r % RING, :, :d_sgu])
        mix_ref[r % RING, :, d_pool:] = (u * (z_ref[r % 2] + sgu_bias_ref[...])).astype(jnp.bfloat16)
        mix = mix_ref[r % RING]
        f = jnp.concatenate([jnp.dot(mix, w_out_ref[j], preferred_element_type=jnp.float32)
                             for j in range(w_out_ref.shape[0])], axis=1)
        y = alpha * x_ref[0, rows(r), :] + gate * f
        o_ref[0, rows(r), :] = _layernorm(y, ln_g_ref[...], ln_b_ref[...])

    in_proj(0)
    for k in range(n_rb + 1):
        if k + 1 < n_rb:
            in_proj(k + 1)
        if k < n_rb:
            mixers(k)
        if k >= 1:
            out_proj(k - 1)
    a_ref[0:POOL_HALO, :] = a_ref[tm:tm + POOL_HALO, :]


def _layer_spec(stacked, l):
    shape = stacked.shape[1:]
    return pl.BlockSpec((None,) + shape, lambda b, i: (l,) + (0,) * len(shape), pipeline_mode=pl.Buffered(1))


def _mod_spec(mod, l):
    return pl.BlockSpec((None, 1) + mod.shape[2:], lambda b, i: (l, b, 0, 0))


def _token_mixing(x, l, mod, w_in, pool_w, pool_scale, sgu_g, sgu_b, sgu_w, sgu_bias, w_out, ln_g, ln_b, alpha):
    n_b, seq, d = x.shape
    d_sgu = sgu_g.shape[-1]
    n_heads = sgu_w.shape[1]
    tm = TM_MIX
    return pl.pallas_call(
        functools.partial(_mix_kernel, alpha),
        grid=(n_b, seq // tm),
        in_specs=[
            pl.BlockSpec((1, tm, d), lambda b, i: (b, i, 0)),
            _mod_spec(mod, l),
        ] + [_layer_spec(p, l) for p in (w_in, pool_w, pool_scale, sgu_g, sgu_b, sgu_w, sgu_bias, w_out, ln_g, ln_b)],
        out_specs=pl.BlockSpec((1, tm, d), lambda b, i: (b, i, 0)),
        out_shape=jax.ShapeDtypeStruct(x.shape, x.dtype),
        scratch_shapes=[
            pltpu.VMEM((POOL_HALO + tm, pool_scale.shape[-1]), jnp.float32),
            pltpu.VMEM((RING, CHUNK, 2 * d_sgu), jnp.float32),
            pltpu.VMEM((RING, CHUNK, w_out.shape[-2]), jnp.bfloat16),
            pltpu.VMEM((2, CHUNK, d_sgu), jnp.float32),
            pltpu.VMEM((n_heads // 2, CHUNK, 2 * CHUNK), jnp.bfloat16),
            pltpu.VMEM((n_heads // 2, 2 * CHUNK, 2 * CHUNK), jnp.bfloat16),
        ],
        compiler_params=pltpu.CompilerParams(
            dimension_semantics=("arbitrary", "arbitrary"),
            vmem_limit_bytes=VMEM_LIMIT_BYTES),
        name="token_mixing",
    )(x, mod, w_in, pool_w, pool_scale, sgu_g, sgu_b, sgu_w, sgu_bias, w_out, ln_g, ln_b)


def _shift_rows(g, prev, k):
    rolled = pltpu.roll(g, k, 0)
    head = lax.broadcasted_iota(jnp.int32, (SUBLANES, g.shape[1]), 0) < k
    top = jnp.where(head, pltpu.roll(prev, k, 0), rolled[:SUBLANES])
    return jnp.concatenate([top, rolled[SUBLANES:]], axis=0)


def _ffn_kernel(alpha, x_ref, mod_ref, w_up_ref, conv_w_ref, conv_b_ref, w_down_ref, ln_g_ref, ln_b_ref,
                o_ref, h_ref, carry_ref, acc_ref):
    i = pl.program_id(1)
    tm = x_ref.shape[1]
    n_chunks, fc, _ = w_down_ref.shape
    d_ff = n_chunks * fc
    rb = FFN_ROW_BLOCK
    n_rb = tm // rb

    @pl.when(i == 0)
    def _():
        carry_ref[...] = jnp.zeros(carry_ref.shape, jnp.float32)

    shift = mod_ref[0, 3:4, :]
    scale = mod_ref[0, 4:5, :]
    gate = mod_ref[0, 5:6, :]

    def rows(r):
        return slice(r * rb, (r + 1) * rb)

    def modulate(r):
        h_ref[rows(r), :] = (x_ref[0, rows(r), :] * (1.0 + scale) + shift).astype(jnp.bfloat16)

    def up_proj(r, c):
        h = h_ref[rows(r), :]
        g = jnp.dot(h, w_up_ref[:, c * fc:(c + 1) * fc], preferred_element_type=jnp.float32)
        val = jnp.dot(h, w_up_ref[:, d_ff + c * fc:d_ff + (c + 1) * fc], preferred_element_type=jnp.float32)
        return g, val

    def activate(c, up):
        g, val = up
        prev = carry_ref[c]
        carry_ref[c] = g[rb - SUBLANES:, :]
        cw = conv_w_ref[c]
        y = (conv_b_ref[c] + _shift_rows(g, prev, 2) * cw[0:1, :] + _shift_rows(g, prev, 1) * cw[1:2, :]
             + g * cw[2:3, :])
        return (_gelu(y) * val).astype(jnp.bfloat16)

    def down_proj(r, c, act):
        d = jnp.dot(act, w_down_ref[c], preferred_element_type=jnp.float32)
        if c == 0:
            acc_ref[rows(r), :] = d
        else:
            acc_ref[rows(r), :] += d

    def finish(r):
        y = alpha * x_ref[0, rows(r), :] + gate * acc_ref[rows(r), :]
        o_ref[0, rows(r), :] = _layernorm(y, ln_g_ref[...], ln_b_ref[...])

    steps = [(r, c) for r in range(n_rb) for c in range(n_chunks)]
    modulate(0)
    pending = up_proj(*steps[0])
    for k, (r, c) in enumerate(steps):
        nxt = None
        if k + 1 < len(steps):
            r1, c1 = steps[k + 1]
            if c1 == 0:
                modulate(r1)
            nxt = up_proj(r1, c1)
        act = activate(c, pending)
        down_proj(r, c, act)
        if c == n_chunks - 1:
            finish(r)
        pending = nxt


def _channel_mixing(x, l, mod, w_up, conv_w, conv_b, w_down, ln_g, ln_b, alpha):
    n_b, seq, d = x.shape
    _, n_chunks, fc, _ = w_down.shape
    tm = TM_FFN
    return pl.pallas_call(
        functools.partial(_ffn_kernel, alpha),
        grid=(n_b, seq // tm),
        in_specs=[
            pl.BlockSpec((1, tm, d), lambda b, i: (b, i, 0)),
            _mod_spec(mod, l),
        ] + [_layer_spec(p, l) for p in (w_up, conv_w, conv_b, w_down, ln_g, ln_b)],
        out_specs=pl.BlockSpec((1, tm, d), lambda b, i: (b, i, 0)),
        out_shape=jax.ShapeDtypeStruct(x.shape, x.dtype),
        scratch_shapes=[
            pltpu.VMEM((tm, d), jnp.bfloat16),
            pltpu.VMEM((n_chunks, SUBLANES, fc), jnp.float32),
            pltpu.VMEM((tm, d), jnp.float32),
        ],
        compiler_params=pltpu.CompilerParams(
            dimension_semantics=("arbitrary", "arbitrary"),
            vmem_limit_bytes=VMEM_LIMIT_BYTES),
        name="channel_mixing",
    )(x, mod, w_up, conv_w, conv_b, w_down, ln_g, ln_b)


def kernel(x, c, ada_w, ada_b, w_in, pool_w, pool_scale, sgu_ln_g, sgu_ln_b, sgu_w, sgu_b, w_out, ln1_g, ln1_b,
           w_up, conv_w, conv_b, w_down, ln2_g, ln2_b):
    n_layers, d, _ = ada_w.shape
    n_b = x.shape[0]
    d_ff = w_down.shape[1]
    n_heads = sgu_w.shape[1]
    n_chunks = d_ff // FF_CHUNK
    alpha = (2.0 * n_layers) ** 0.25
    bf16 = jnp.bfloat16

    mod = _modulation(c, ada_w, ada_b).reshape(n_layers, n_b, N_MOD, d)

    w_in_b = w_in.astype(bf16)
    n_groups, gdim = pool_w.shape[1], pool_w.shape[2]
    pool_pairs = jnp.zeros((n_layers, n_groups // 2, 2 * gdim, 2 * gdim), pool_w.dtype)
    for g in range(n_groups):
        q = g % 2
        pool_pairs = pool_pairs.at[:, g // 2, q * gdim:(q + 1) * gdim, q * gdim:(q + 1) * gdim].set(pool_w[:, g])
    pool_w_b = pool_pairs.astype(bf16)
    d_mix = w_out.shape[1]
    w_out_b = jnp.transpose(w_out.reshape(n_layers, d_mix, N_HALVES, d // N_HALVES), (0, 2, 1, 3)).astype(bf16)
    w_up_b = w_up.astype(bf16)
    w_down_b = w_down.reshape(n_layers, n_chunks, FF_CHUNK, d).astype(bf16)
    conv_w_c = jnp.transpose(conv_w.reshape(n_layers, CONV_WIDTH, n_chunks, FF_CHUNK), (0, 2, 1, 3))
    conv_b_c = conv_b.reshape(n_layers, n_chunks, 1, FF_CHUNK)
    hdim = sgu_ln_g.shape[1] // n_heads
    sgu_bias = jnp.broadcast_to(jnp.transpose(sgu_b, (0, 2, 1))[:, :, :, None],
                                (n_layers, CHUNK, n_heads, hdim)).reshape(n_layers, CHUNK, n_heads * hdim)

    rows = lambda p: p.reshape(n_layers, 1, -1)
    for l in range(n_layers):
        x = _token_mixing(x, l, mod, w_in_b, pool_w_b, rows(pool_scale), rows(sgu_ln_g), rows(sgu_ln_b), sgu_w,
                          sgu_bias, w_out_b, rows(ln1_g), rows(ln1_b), alpha)
        x = _channel_mixing(x, l, mod, w_up_b, conv_w_c, conv_b_c, w_down_b, rows(ln2_g), rows(ln2_b), alpha)
    return x
```
